```python
import jax, jax.numpy as jnp
from jax import lax
import numpy as np

D_MODEL = 1024
BATCH = 16
SEQ = 4096
DEPTH = 2
DEC_BATCH = 32
DEC_SEQ = 64
PAST_LEN = 1024

CHUNK = 64
N_A = DEPTH // 2
N_B = DEPTH - N_A
RET_HEADS = 4
RET_DK = D_MODEL // RET_HEADS
RET_DV = 2 * RET_DK
RET_V = RET_HEADS * RET_DV
FOX_HEADS = 16
FOX_DH = D_MODEL // FOX_HEADS
FOX_W = FOX_HEADS * FOX_DH
D_FF = 2816
Q_BLOCK = 128
ROPE_BASE = 10000.0
EPS = 1e-6

kernel_name = "yoco_retention_fox_macaron_stream"


def rmsnorm(x, g):
    xf = x.astype(jnp.float32)
    y = xf * lax.rsqrt(jnp.mean(xf * xf, axis=-1, keepdims=True) + EPS) * g.astype(jnp.float32)
    return y.astype(x.dtype)


def swiglu_ffn(h, g, w_in, w_out):
    gate, up = jnp.split(rmsnorm(h, g) @ w_in, 2, axis=-1)
    return (jax.nn.silu(gate) * up) @ w_out


def rope(x, pos):
    half = x.shape[-1] // 2
    inv = ROPE_BASE ** (-jnp.arange(half, dtype=jnp.float32) / half)
    ang = pos.astype(jnp.float32)[:, None] * inv[None, :]
    cos, sin = jnp.cos(ang)[None, :, None, :], jnp.sin(ang)[None, :, None, :]
    xf = x.astype(jnp.float32)
    x1, x2 = xf[..., :half], xf[..., half:]
    return jnp.concatenate([x1 * cos - x2 * sin, x1 * sin + x2 * cos], axis=-1)


def retention_log_gamma():
    return jnp.log1p(-jnp.exp2(-5.0 - jnp.arange(RET_HEADS, dtype=jnp.float32)))


def retention_chunk(S, q, k, v, log_gamma):
    c = q.shape[1]
    pos = jnp.arange(c, dtype=jnp.float32)
    dist = jnp.abs(pos[:, None] - pos[None, :])
    decay_intra = jnp.exp(log_gamma[:, None, None] * dist)
    scores = jnp.einsum('bchd,bshd->bhcs', q, k) * decay_intra[None]
    intra = jnp.einsum('bhcs,bshe->bche', scores, v)
    q_decay = jnp.exp(log_gamma[None, :] * (pos[:, None] + 1.0))
    inter = jnp.einsum('bchd,bhde->bche', q, S) * q_decay[None, :, :, None]
    k_decay = jnp.exp(log_gamma[None, :] * (c - 1.0 - pos[:, None]))
    S_new = jnp.exp(log_gamma * c)[None, :, None, None] * S + jnp.einsum(
        'bchd,bche->bhde', k * k_decay[None, :, :, None], v)
    return S_new, intra + inter


def retention_mixer(h, S0, pos, chunk_len, w_in, gn_g, w_out):
    B, L, _ = h.shape
    q, k, v, g = jnp.split(h @ w_in, [D_MODEL, 2 * D_MODEL, 2 * D_MODEL + RET_V], axis=-1)
    q = rope(q.reshape(B, L, RET_HEADS, RET_DK), pos) * (RET_DK ** -0.5)
    k = rope(k.reshape(B, L, RET_HEADS, RET_DK), pos)
    v = v.reshape(B, L, RET_HEADS, RET_DV).astype(jnp.float32)
    n = L // chunk_len
    to_chunks = lambda t: jnp.moveaxis(t.reshape(B, n, chunk_len, *t.shape[2:]), 1, 0)
    log_gamma = retention_log_gamma()
    S_fin, o = lax.scan(lambda S, qkv: retention_chunk(S, qkv[0], qkv[1], qkv[2], log_gamma),
                        S0.astype(jnp.float32), (to_chunks(q), to_chunks(k), to_chunks(v)))
    o = jnp.moveaxis(o, 0, 1).reshape(B, L, RET_HEADS, RET_DV)
    mu = jnp.mean(o, axis=-1, keepdims=True)
    var = jnp.mean(jnp.square(o - mu), axis=-1, keepdims=True)
    o = ((o - mu) * lax.rsqrt(var + EPS)).reshape(B, L, RET_V) * gn_g.astype(jnp.float32)
    o = (jax.nn.silu(g.astype(jnp.float32)) * o).astype(h.dtype)
    return o @ w_out, S_fin


def fox_kv(h, kv_g, w_kvf, b_f):
    B, L, _ = h.shape
    k, v, fl = jnp.split(rmsnorm(h, kv_g) @ w_kvf, [FOX_W, 2 * FOX_W], axis=-1)
    logf = jax.nn.log_sigmoid(fl.astype(jnp.float32) + b_f.astype(jnp.float32))
    return (k.reshape(B, L, FOX_HEADS, FOX_DH), v.reshape(B, L, FOX_HEADS, FOX_DH), logf)


def fox_block(q_blk, c_blk, qpos, k, v, cumf, kpos):
    s = jnp.einsum('bqhd,bkhd->bhqk', q_blk, k).astype(jnp.float32) * (FOX_DH ** -0.5)
    s = s + jnp.swapaxes(c_blk, 1, 2)[..., None] - jnp.swapaxes(cumf, 1, 2)[:, :, None, :]
    s = jnp.where((qpos[:, None] >= kpos[None, :])[None, None], s, -jnp.inf)
    p = jax.nn.softmax(s, axis=-1)
    return jnp.einsum('bhqk,bkhd->bqhd', p.astype(v.dtype), v)


def fox_mixer(h, w_q, w_out, k, v, cumf, kpos, qpos, c_q):
    B, L, _ = h.shape
    q = (h @ w_q).reshape(B, L, FOX_HEADS, FOX_DH)
    nb = max(L // Q_BLOCK, 1)
    qb = L // nb
    blocks = (jnp.moveaxis(q.reshape(B, nb, qb, FOX_HEADS, FOX_DH), 1, 0),
              jnp.moveaxis(c_q.reshape(B, nb, qb, FOX_HEADS), 1, 0),
              qpos.reshape(nb, qb))
    o = lax.map(lambda a: fox_block(a[0], a[1], a[2], k, v, cumf, kpos), blocks)
    o = jnp.moveaxis(o, 0, 1).reshape(B, L, FOX_W)
    return o @ w_out


def trunk(x, pos, chunk_len, ret_S0, past, p):
    h = x
    L = x.shape[1]
    ret_states = []
    k_new = v_new = logf_new = None
    k_all = v_all = cumf_all = kpos = c_q = None
    for layer in range(DEPTH):
        h = h + 0.5 * swiglu_ffn(h, p['ffn1_g'][layer], p['ffn1_w_in'][layer], p['ffn1_w_out'][layer])
        hn = rmsnorm(h, p['mix_g'][layer])
        if layer < N_A:
            y, S = retention_mixer(hn, ret_S0[layer], pos, chunk_len,
                                   p['ret_w_in'][layer], p['ret_gn_g'][layer], p['ret_w_out'][layer])
            ret_states.append(S)
        else:
            j = layer - N_A
            y = fox_mixer(hn, p['fox_w_q'][j], p['fox_w_out'][j], k_all, v_all, cumf_all, kpos, pos, c_q)
        h = h + y
        h = h + 0.5 * swiglu_ffn(h, p['ffn2_g'][layer], p['ffn2_w_in'][layer], p['ffn2_w_out'][layer])
        if layer == N_A - 1:
            k_new, v_new, logf_new = fox_kv(h, p['kv_g'], p['fox_w_kvf'], p['fox_b_f'])
            if past is None:
                k_all, v_all, logf_all = k_new, v_new, logf_new
            else:
                k_all = jnp.concatenate([past[0].astype(k_new.dtype), k_new], axis=1)
                v_all = jnp.concatenate([past[1].astype(v_new.dtype), v_new], axis=1)
                logf_all = jnp.concatenate([past[2].astype(jnp.float32), logf_new], axis=1)
            cumf_all = jnp.cumsum(logf_all, axis=1)
            n_keys = k_all.shape[1]
            kpos = jnp.arange(n_keys)
            c_q = cumf_all[:, n_keys - L:]
    return (rmsnorm(h, p['final_g']), jnp.stack(ret_states), k_new, v_new, logf_new)


def setup_inputs(seed: int = 0) -> dict:
    key = jax.random.key(seed)
    ks = jax.random.split(key, 24)
    nrm = lambda k, shape, scale: jax.random.normal(k, shape, jnp.float32) * scale
    gain = lambda k, shape: 1.0 + 0.05 * jax.random.normal(k, shape, jnp.float32)
    return {
        'x_prompt': nrm(ks[0], (BATCH, SEQ, D_MODEL), 1.0),
        'x_sample': nrm(ks[1], (DEC_BATCH, DEC_SEQ, D_MODEL), 1.0),
        'state_ret': nrm(ks[2], (N_A, DEC_BATCH, RET_HEADS, RET_DK, RET_DV), 0.1),
        'cache_k': nrm(ks[3], (DEC_BATCH, PAST_LEN, FOX_HEADS, FOX_DH), 1.0),
        'cache_v': nrm(ks[4], (DEC_BATCH, PAST_LEN, FOX_HEADS, FOX_DH), 1.0),
        'cache_logf': jax.nn.log_sigmoid(3.0 + nrm(ks[5], (DEC_BATCH, PAST_LEN, FOX_HEADS), 1.0)),
        'ffn1_g': gain(ks[6], (DEPTH, D_MODEL)),
        'ffn1_w_in': nrm(ks[7], (DEPTH, D_MODEL, 2 * D_FF), D_MODEL ** -0.5),
        'ffn1_w_out': nrm(ks[8], (DEPTH, D_FF, D_MODEL), D_FF ** -0.5),
        'mix_g': gain(ks[9], (DEPTH, D_MODEL)),
        'ffn2_g': gain(ks[10], (DEPTH, D_MODEL)),
        'ffn2_w_in': nrm(ks[11], (DEPTH, D_MODEL, 2 * D_FF), D_MODEL ** -0.5),
        'ffn2_w_out': nrm(ks[12], (DEPTH, D_FF, D_MODEL), D_FF ** -0.5),
        'ret_w_in': nrm(ks[13], (N_A, D_MODEL, 2 * D_MODEL + 2 * RET_V), D_MODEL ** -0.5),
        'ret_gn_g': gain(ks[14], (N_A, RET_V)),
        'ret_w_out': nrm(ks[15], (N_A, RET_V, D_MODEL), RET_V ** -0.5),
        'kv_g': gain(ks[16], (D_MODEL,)),
        'fox_w_kvf': nrm(ks[17], (D_MODEL, 2 * FOX_W + FOX_HEADS), D_MODEL ** -0.5),
        'fox_b_f': jnp.linspace(1.0, 5.0, FOX_HEADS, dtype=jnp.float32) + nrm(ks[18], (FOX_HEADS,), 0.1),
        'fox_w_q': nrm(ks[19], (N_B, D_MODEL, FOX_W), D_MODEL ** -0.5),
        'fox_w_out': nrm(ks[20], (N_B, FOX_W, D_MODEL), FOX_W ** -0.5),
        'final_g': gain(ks[21], (D_MODEL,)),
    }


def reference(x_prompt, x_sample, state_ret, cache_k, cache_v, cache_logf,
              ffn1_g, ffn1_w_in, ffn1_w_out, mix_g, ffn2_g, ffn2_w_in, ffn2_w_out,
              ret_w_in, ret_gn_g, ret_w_out, kv_g, fox_w_kvf, fox_b_f, fox_w_q, fox_w_out, final_g):
    p = {'ffn1_g': ffn1_g, 'ffn1_w_in': ffn1_w_in, 'ffn1_w_out': ffn1_w_out, 'mix_g': mix_g,
         'ffn2_g': ffn2_g, 'ffn2_w_in': ffn2_w_in, 'ffn2_w_out': ffn2_w_out,
         'ret_w_in': ret_w_in, 'ret_gn_g': ret_gn_g, 'ret_w_out': ret_w_out,
         'kv_g': kv_g, 'fox_w_kvf': fox_w_kvf, 'fox_b_f': fox_b_f,
         'fox_w_q': fox_w_q, 'fox_w_out': fox_w_out, 'final_g': final_g}
    b_p, l_p = x_prompt.shape[0], x_prompt.shape[1]
    l_s = x_sample.shape[1]
    past_len = cache_k.shape[1]
    S0_prompt = jnp.zeros((N_A, b_p, RET_HEADS, RET_DK, RET_DV), jnp.float32)
    y_prompt, state_ret_prompt, k_prompt, v_prompt, logf_prompt = trunk(
        x_prompt, jnp.arange(l_p), CHUNK, S0_prompt, None, p)
    y_sample, state_ret_sample, k_sample, v_sample, logf_sample = trunk(
        x_sample, past_len + jnp.arange(l_s), l_s, state_ret, (cache_k, cache_v, cache_logf), p)
    return (y_prompt, y_sample, state_ret_prompt, k_prompt, v_prompt, logf_prompt,
            state_ret_sample, k_sample, v_sample, logf_sample)
```

```python
import functools

import jax
import jax.numpy as jnp
from jax import lax
from jax.experimental import pallas as pl
from jax.experimental.pallas import tpu as pltpu

F32 = jnp.float32
BF16 = jnp.bfloat16

EPS = 1e-6
ROPE_BASE = 10000.0
RET_CHUNK = 64
LANES = 128
HALF_LANES = LANES // 2
V7X_VMEM_LIMIT = 56 * 1024 * 1024

AUG_PIECE0 = 64
AUG_QPIECE0 = 112
N_PIECES = 3


def _cparams(sem):
    return pltpu.CompilerParams(dimension_semantics=sem, vmem_limit_bytes=V7X_VMEM_LIMIT)


def _resident(shape):
    nd = len(shape)
    return pl.BlockSpec(shape, lambda *_: (0,) * nd, pipeline_mode=pl.Buffered(1))


def _rmsnorm(x, g):
    ms = jnp.mean(x * x, axis=-1, keepdims=True)
    return x * lax.rsqrt(ms + EPS) * g


def _silu(x):
    return x * jax.nn.sigmoid(x)


def _dot(a, b):
    return jnp.dot(a, b, preferred_element_type=F32)


def _dot_nt(a, b):
    return lax.dot_general(a, b, (((1,), (1,)), ((), ())), preferred_element_type=F32)


def _dot_tn(a, b):
    return lax.dot_general(a, b, (((0,), (0,)), ((), ())), preferred_element_type=F32)


def _split3(c):
    hi = c.astype(BF16).astype(F32)
    r = c - hi
    mid = r.astype(BF16).astype(F32)
    lo = (r - mid).astype(BF16).astype(F32)
    return hi, mid, lo


def _row_tile(n, want):
    t = min(n, want)
    assert n % t == 0 and t % 8 == 0, (n, t)
    return t


def _ffn_kernel(x_ref, g_ref, win_ref, wout_ref, fg_ref, o_ref, act_ref, *, d_ff, ck, final_norm):
    x = x_ref[...]
    xn = _rmsnorm(x, g_ref[...]).astype(BF16)
    for c in range(d_ff // ck):
        gate = _dot(xn, win_ref[:, c * ck:(c + 1) * ck])
        up = _dot(xn, win_ref[:, d_ff + c * ck:d_ff + (c + 1) * ck])
        act_ref[:, c * ck:(c + 1) * ck] = (_silu(gate) * up).astype(BF16)
    y = x + 0.5 * _dot(act_ref[...], wout_ref[...])
    if final_norm:
        y = _rmsnorm(y, fg_ref[...])
    o_ref[...] = y


def _ffn(h, g, w_in, w_out, final_g=None):
    n, d = h.shape
    d_ff = w_out.shape[0]
    tm = _row_tile(n, 512)
    ck = 256
    assert d_ff % ck == 0
    fg = jnp.ones((d,), F32) if final_g is None else final_g
    return pl.pallas_call(
        functools.partial(_ffn_kernel, d_ff=d_ff, ck=ck, final_norm=final_g is not None),
        out_shape=jax.ShapeDtypeStruct((n, d), F32),
        grid=(n // tm,),
        in_specs=[
            pl.BlockSpec((tm, d), lambda i: (i, 0)),
            _resident((1, d)),
            _resident((d, 2 * d_ff)),
            _resident((d_ff, d)),
            _resident((1, d)),
        ],
        out_specs=pl.BlockSpec((tm, d), lambda i: (i, 0)),
        scratch_shapes=[pltpu.VMEM((tm, d_ff), BF16)],
        compiler_params=_cparams(("parallel",)),
        name="ffn",
    )(h, g.reshape(1, d), w_in.astype(BF16), w_out.astype(BF16), fg.reshape(1, d))


def _proj_res_kernel(a_ref, w_ref, r_ref, o_ref):
    o_ref[...] = r_ref[...] + _dot(a_ref[...], w_ref[...])


def _proj_res(a, w, res):
    n, k = a.shape
    d = w.shape[1]
    tm = _row_tile(n, 512)
    return pl.pallas_call(
        _proj_res_kernel,
        out_shape=jax.ShapeDtypeStruct((n, d), F32),
        grid=(n // tm,),
        in_specs=[
            pl.BlockSpec((tm, k), lambda i: (i, 0)),
            _resident((k, d)),
            pl.BlockSpec((tm, d), lambda i: (i, 0)),
        ],
        out_specs=pl.BlockSpec((tm, d), lambda i: (i, 0)),
        compiler_params=_cparams(("parallel",)),
        name="proj_res",
    )(a, w.astype(BF16), res)


def _rope_tables(pos, half):
    inv = ROPE_BASE ** (-jnp.arange(half, dtype=F32) / half)
    ang = pos.astype(F32)[:, None] * inv[None, :]
    return jnp.cos(ang), jnp.sin(ang)


def _ret_proj_kernel(x_ref, g_ref, w_ref, cos_ref, sin_ref, q_ref, k_ref, v_ref, sg_ref, *, d, nh, rv):
    dk = d // nh
    half = dk // 2
    xn = _rmsnorm(x_ref[...], g_ref[...]).astype(BF16)
    cos = cos_ref[...]
    sin = sin_ref[...]
    for off, out_ref, scale in ((0, q_ref, dk ** -0.5), (d, k_ref, 1.0)):
        t = _dot(xn, w_ref[:, off:off + d])
        for hd in range(nh):
            x1 = t[:, hd * dk:hd * dk + half]
            x2 = t[:, hd * dk + half:(hd + 1) * dk]
            out_ref[:, hd * dk:hd * dk + half] = ((x1 * cos - x2 * sin) * scale).astype(BF16)
            out_ref[:, hd * dk + half:(hd + 1) * dk] = ((x1 * sin + x2 * cos) * scale).astype(BF16)
    v_ref[...] = _dot(xn, w_ref[:, 2 * d:2 * d + rv]).astype(BF16)
    sg_ref[...] = _silu(_dot(xn, w_ref[:, 2 * d + rv:2 * d + 2 * rv])).astype(BF16)


def _ret_proj(h, g, w_in, pos, seq_len, nh, rv):
    n, d = h.shape
    half = d // nh // 2
    tm = _row_tile(n, 512)
    cos, sin = _rope_tables(pos, half)
    if seq_len % tm == 0:
        period = seq_len // tm
    else:
        assert tm % seq_len == 0
        cos, sin = (jnp.tile(t, (tm // seq_len, 1)) for t in (cos, sin))
        period = 1
    tab = pl.BlockSpec((tm, half), lambda i: (i % period, 0))
    row = lambda w: pl.BlockSpec((tm, w), lambda i: (i, 0))
    return pl.pallas_call(
        functools.partial(_ret_proj_kernel, d=d, nh=nh, rv=rv),
        out_shape=(jax.ShapeDtypeStruct((n, d), BF16), jax.ShapeDtypeStruct((n, d), BF16),
                   jax.ShapeDtypeStruct((n, rv), BF16), jax.ShapeDtypeStruct((n, rv), BF16)),
        grid=(n // tm,),
        in_specs=[row(d), _resident((1, d)), _resident((d, 2 * d + 2 * rv)), tab, tab],
        out_specs=(row(d), row(d), row(rv), row(rv)),
        compiler_params=_cparams(("parallel",)),
        name="ret_proj",
    )(h, g.reshape(1, d), w_in.astype(BF16), cos, sin)


def _ret_decay_tables(nh, t, chunk):
    log_gamma = jnp.log1p(-jnp.exp2(-5.0 - jnp.arange(nh, dtype=F32)))
    pos = jnp.arange(t, dtype=F32)
    dist = jnp.abs(pos[:, None] - pos[None, :])
    cid = jnp.arange(t) // chunk
    visible = (cid[None, :] <= cid[:, None]).astype(F32)
    dmask = jnp.exp(log_gamma[:, None, None] * dist) * visible[None]
    qdec = jnp.exp(log_gamma[:, None] * (pos[None, :] + 1.0))[:, :, None]
    kdec = jnp.exp(log_gamma[:, None] * (t - 1.0 - pos[None, :]))[:, :, None]
    sdec = jnp.exp(log_gamma * t)
    return dmask, qdec, kdec, sdec


def _retention_kernel(sdec_ref, q_ref, k_ref, v_ref, sg_ref, dmask_ref, qdec_ref, kdec_ref, gn_ref, s0_ref,
                      o_ref, sout_ref, s_ref, *, nh, dk, dv):
    j = pl.program_id(1)

    @pl.when(j == 0)
    def _():
        s_ref[...] = s0_ref[...]

    for hd in range(nh):
        qh = q_ref[:, hd * dk:(hd + 1) * dk]
        kh = k_ref[:, hd * dk:(hd + 1) * dk]
        vh = v_ref[:, hd * dv:(hd + 1) * dv]
        state = s_ref[hd]
        scores = _dot_nt(qh, kh) * dmask_ref[hd]
        o = _dot(scores.astype(BF16), vh) + _dot(qh, state.astype(BF16)) * qdec_ref[hd]
        kd = (kh.astype(F32) * kdec_ref[hd]).astype(BF16)
        s_ref[hd] = sdec_ref[hd] * state + _dot_tn(kd, vh)
        mu = jnp.mean(o, axis=-1, keepdims=True)
        oc = o - mu
        var = jnp.mean(oc * oc, axis=-1, keepdims=True)
        on = oc * lax.rsqrt(var + EPS) * gn_ref[:, hd * dv:(hd + 1) * dv]
        o_ref[:, hd * dv:(hd + 1) * dv] = (sg_ref[:, hd * dv:(hd + 1) * dv].astype(F32) * on).astype(BF16)

    @pl.when(j == pl.num_programs(1) - 1)
    def _():
        sout_ref[...] = s_ref[...]


def _retention(q, k, v, sg, s0, gn_g, batch, seq_len, chunk):
    nh, dk, dv = s0.shape[1:]
    rv = nh * dv
    t = _row_tile(seq_len, max(chunk, 256))
    assert t % chunk == 0
    nblk = seq_len // t
    dmask, qdec, kdec, sdec = _ret_decay_tables(nh, t, chunk)
    row = lambda w: pl.BlockSpec((t, w), lambda b, j: (b * nblk + j, 0))
    st = pl.BlockSpec((None, nh, dk, dv), lambda b, j: (b, 0, 0, 0))
    return pl.pallas_call(
        functools.partial(_retention_kernel, nh=nh, dk=dk, dv=dv),
        out_shape=(jax.ShapeDtypeStruct((batch * seq_len, rv), BF16),
                   jax.ShapeDtypeStruct((batch, nh, dk, dv), F32)),
        grid=(batch, nblk),
        in_specs=[
            pl.BlockSpec(memory_space=pltpu.SMEM),
            row(nh * dk), row(nh * dk), row(rv), row(rv),
            _resident((nh, t, t)), _resident((nh, t, 1)), _resident((nh, t, 1)), _resident((1, rv)),
            st,
        ],
        out_specs=(row(rv), st),
        scratch_shapes=[pltpu.VMEM((nh, dk, dv), F32)],
        compiler_params=_cparams(("parallel", "arbitrary")),
        name="retention",
    )(sdec, q, k, v, sg, dmask, qdec, kdec, gn_g.reshape(1, rv), s0)


def _fox_kv_kernel(x_ref, g_ref, wk_ref, wv_ref, wf_ref, bf_ref, k_ref, v_ref, logf_ref):
    xn = _rmsnorm(x_ref[...], g_ref[...]).astype(BF16)
    k_ref[...] = _dot(xn, wk_ref[...])
    v_ref[...] = _dot(xn, wv_ref[...])
    z = _dot(xn, wf_ref[...]) + bf_ref[...]
    logf_ref[...] = jnp.minimum(z, 0.0) - jnp.log1p(jnp.exp(-jnp.abs(z)))


def _fox_kv(h, g, w_kvf, b_f, width, nh):
    n, d = h.shape
    tm = _row_tile(n, 512)
    wf = jnp.zeros((d, LANES), BF16).at[:, :nh].set(w_kvf[:, 2 * width:].astype(BF16))
    bf = jnp.zeros((1, LANES), F32).at[0, :nh].set(b_f)
    row = lambda w: pl.BlockSpec((tm, w), lambda i: (i, 0))
    return pl.pallas_call(
        _fox_kv_kernel,
        out_shape=(jax.ShapeDtypeStruct((n, width), F32), jax.ShapeDtypeStruct((n, width), F32),
                   jax.ShapeDtypeStruct((n, LANES), F32)),
        grid=(n // tm,),
        in_specs=[row(d), _resident((1, d)), _resident((d, width)), _resident((d, width)),
                  _resident((d, LANES)), _resident((1, LANES))],
        out_specs=(row(width), row(width), row(LANES)),
        compiler_params=_cparams(("parallel",)),
        name="fox_kv",
    )(h, g.reshape(1, d), w_kvf[:, :width].astype(BF16), w_kvf[:, width:2 * width].astype(BF16), wf, bf)


def _cumsum_kernel(x_ref, tri_ref, o_ref, *, tc, nblk):
    tri = tri_ref[...]

    def body(i, carry):
        r0 = pl.multiple_of(i * tc, tc)
        hi, mid, lo = _split3(x_ref[pl.ds(r0, tc), :])
        c = _dot(tri, hi.astype(BF16)) + _dot(tri, mid.astype(BF16)) + _dot(tri, lo.astype(BF16)) + carry
        o_ref[pl.ds(r0, tc), :] = c
        return c[tc - 1:tc, :]

    lax.fori_loop(0, nblk, body, jnp.zeros((1, x_ref.shape[-1]), F32))


def _cumsum_rows(x):
    b, n, w = x.shape
    tc = 64
    assert n % tc == 0
    tri = (jnp.arange(tc)[:, None] >= jnp.arange(tc)[None, :]).astype(BF16)
    blk = pl.BlockSpec((None, n, w), lambda i: (i, 0, 0))
    return pl.pallas_call(
        functools.partial(_cumsum_kernel, tc=tc, nblk=n // tc),
        out_shape=jax.ShapeDtypeStruct((b, n, w), F32),
        grid=(b,),
        in_specs=[blk, _resident((tc, tc))],
        out_specs=blk,
        compiler_params=_cparams(("parallel",)),
        name="cumsum",
    )(x, tri)


def _lane_iota(rows):
    return lax.broadcasted_iota(jnp.int32, (rows, LANES), 1)


def _head_lanes(x_ref, hd):
    t = x_ref[:, (hd // 2) * LANES:(hd // 2 + 1) * LANES]
    return pltpu.roll(t, HALF_LANES, axis=1) if hd % 2 else t


def _kpack_kernel(k_ref, v_ref, c_ref, ka_ref, vp_ref, *, nh):
    rows = k_ref.shape[0]
    lane = _lane_iota(rows)
    hi, mid, lo = _split3(c_ref[...])
    extra = jnp.where(lane < AUG_QPIECE0 + N_PIECES, 1.0, 0.0)
    for p, piece in reversed(list(enumerate((hi, mid, lo)))):
        lo_lane = AUG_PIECE0 + p * nh
        extra = jnp.where(lane < lo_lane + nh, -pltpu.roll(piece, lo_lane, axis=1), extra)
    for hd in range(nh):
        ka_ref[hd] = jnp.where(lane < HALF_LANES, _head_lanes(k_ref, hd), extra).astype(BF16)
    for p in range(nh // 2):
        vp_ref[p] = v_ref[:, p * LANES:(p + 1) * LANES].astype(BF16)


def _kpack(k, v, c, nh):
    b, n, width = k.shape
    assert width == nh * HALF_LANES and N_PIECES * nh <= AUG_QPIECE0 - AUG_PIECE0
    assert nh & (nh - 1) == 0 and AUG_PIECE0 % nh == 0
    tm = 512 if n % 512 == 0 else n
    row = lambda w: pl.BlockSpec((None, tm, w), lambda i, j: (i, j, 0))
    hblk = lambda g: pl.BlockSpec((None, g, tm, LANES), lambda i, j: (i, 0, j, 0))
    return pl.pallas_call(
        functools.partial(_kpack_kernel, nh=nh),
        out_shape=(jax.ShapeDtypeStruct((b, nh, n, LANES), BF16),
                   jax.ShapeDtypeStruct((b, nh // 2, n, LANES), BF16)),
        grid=(b, n // tm),
        in_specs=[row(width), row(width), row(LANES)],
        out_specs=(hblk(nh), hblk(nh // 2)),
        compiler_params=_cparams(("parallel", "parallel")),
        name="kpack",
    )(k, v, c)


def _fox_q_kernel(x_ref, g_ref, w_ref, c_ref, qa_ref, *, nh):
    rows = x_ref.shape[0]
    lane = _lane_iota(rows)
    xn = _rmsnorm(x_ref[...], g_ref[...]).astype(BF16)
    q = _dot(xn, w_ref[...]) * (HALF_LANES ** -0.5)
    pieces = _split3(c_ref[...])
    in_pieces = (lane >= AUG_PIECE0) & (lane < AUG_PIECE0 + N_PIECES * nh)
    for hd in range(nh):
        extra = jnp.where(in_pieces & ((lane & (nh - 1)) == hd), 1.0, 0.0)
        for p, piece in enumerate(pieces):
            extra = jnp.where(lane == AUG_QPIECE0 + p, pltpu.roll(piece, AUG_QPIECE0 + p - hd, axis=1), extra)
        t = q[:, (hd // 2) * LANES:(hd // 2 + 1) * LANES]
        t = pltpu.roll(t, HALF_LANES, axis=1) if hd % 2 else t
        qa_ref[hd] = jnp.where(lane < HALF_LANES, t, extra).astype(BF16)


def _fox_q(h, g, w_q, cq, batch, seq_len, nh):
    n, d = h.shape
    tm = _row_tile(seq_len, 512)
    nblk = seq_len // tm
    return pl.pallas_call(
        functools.partial(_fox_q_kernel, nh=nh),
        out_shape=jax.ShapeDtypeStruct((batch, nh, seq_len, LANES), BF16),
        grid=(batch, nblk),
        in_specs=[
            pl.BlockSpec((tm, d), lambda b, j: (b * nblk + j, 0)),
            _resident((1, d)),
            _resident((d, nh * HALF_LANES)),
            pl.BlockSpec((None, tm, LANES), lambda b, j: (b, j, 0)),
        ],
        out_specs=pl.BlockSpec((None, nh, tm, LANES), lambda b, j: (b, 0, j, 0)),
        compiler_params=_cparams(("parallel", "parallel")),
        name="fox_q",
    )(h, g.reshape(1, d), w_q.astype(BF16), cq)


def _fox_attn_kernel(q_ref, k_ref, v_ref, o_ref, *, tq, tk, q_off):
    i = pl.program_id(2)
    q0 = q_off + i * tq
    n_full = (q0 + 1) // tk
    n_any = (q0 + tq - 1) // tk + 1
    row = lax.broadcasted_iota(jnp.int32, (tq, tk), 0)
    col = lax.broadcasted_iota(jnp.int32, (tq, tk), 1)
    outs = []
    for hh in range(2):
        q = q_ref[hh]

        def step(j, carry, masked, hh=hh, q=q):
            m, l, acc = carry
            r0 = pl.multiple_of(j * tk, tk)
            s = _dot_nt(q, k_ref[hh, pl.ds(r0, tk), :])
            if masked:
                s = jnp.where(q0 + row >= r0 + col, s, -jnp.inf)
            m_new = jnp.maximum(m, jnp.max(s, axis=-1, keepdims=True))
            alpha = jnp.exp(m - m_new)
            p = jnp.exp(s - m_new)
            l = alpha * l + jnp.sum(p, axis=-1, keepdims=True)
            acc = alpha * acc + _dot(p.astype(BF16), v_ref[pl.ds(r0, tk), :])
            return m_new, l, acc

        carry = (jnp.full((tq, 1), -jnp.inf, F32), jnp.zeros((tq, 1), F32), jnp.zeros((tq, LANES), F32))
        carry = lax.fori_loop(0, n_full, functools.partial(step, masked=False), carry)
        _, l, acc = lax.fori_loop(n_full, n_any, functools.partial(step, masked=True), carry)
        outs.append(acc / l)
    lane = _lane_iota(tq)
    o_ref[...] = jnp.where(lane < HALF_LANES, outs[0], outs[1]).astype(BF16)


def _fox_attn(qa, ka, vp, q_off):
    b, nh, seq_len, _ = qa.shape
    n_keys = ka.shape[2]
    tq = _row_tile(seq_len, 256)
    tk = tq
    assert n_keys % tk == 0 and q_off + seq_len == n_keys
    nq = seq_len // tq
    return pl.pallas_call(
        functools.partial(_fox_attn_kernel, tq=tq, tk=tk, q_off=q_off),
        out_shape=jax.ShapeDtypeStruct((b * seq_len, nh * HALF_LANES), BF16),
        grid=(b, nh // 2, nq),
        in_specs=[
            pl.BlockSpec((None, 2, tq, LANES), lambda bi, p, i: (bi, p, i, 0)),
            pl.BlockSpec((None, 2, n_keys, LANES), lambda bi, p, i: (bi, p, 0, 0)),
            pl.BlockSpec((None, None, n_keys, LANES), lambda bi, p, i: (bi, p, 0, 0)),
        ],
        out_specs=pl.BlockSpec((tq, LANES), lambda bi, p, i: (bi * nq + i, p)),
        compiler_params=_cparams(("parallel", "parallel", "arbitrary")),
        name="fox_attn",
    )(qa, ka, vp)


def _trunk(x, pos0, chunk, s0, past, p):
    batch, seq_len, d = x.shape
    nh_ret, dk, dv = s0.shape[2:]
    rv = nh_ret * dv
    nh_fox = p['fox_b_f'].shape[0]
    width = p['fox_w_q'].shape[-1]
    pos = pos0 + jnp.arange(seq_len)
    h = x.reshape(batch * seq_len, d)

    h = _ffn(h, p['ffn1_g'][0], p['ffn1_w_in'][0], p['ffn1_w_out'][0])
    q, k, v, sg = _ret_proj(h, p['mix_g'][0], p['ret_w_in'][0], pos, seq_len, nh_ret, rv)
    o, s_fin = _retention(q, k, v, sg, s0[0], p['ret_gn_g'][0], batch, seq_len, chunk)
    h = _proj_res(o, p['ret_w_out'][0], h)
    h = _ffn(h, p['ffn2_g'][0], p['ffn2_w_in'][0], p['ffn2_w_out'][0])

    k_new, v_new, logf_pad = _fox_kv(h, p['kv_g'], p['fox_w_kvf'], p['fox_b_f'], width, nh_fox)
    k_new = k_new.reshape(batch, seq_len, width)
    v_new = v_new.reshape(batch, seq_len, width)
    logf_new = logf_pad.reshape(batch, seq_len, LANES)[:, :, :nh_fox]
    if past is None:
        k_all, v_all, logf_all = k_new, v_new, logf_new
    else:
        past_len = past[0].shape[1]
        k_all = jnp.concatenate([past[0].reshape(batch, past_len, width), k_new], axis=1)
        v_all = jnp.concatenate([past[1].reshape(batch, past_len, width), v_new], axis=1)
        logf_all = jnp.concatenate([past[2], logf_new], axis=1)
    n_keys = k_all.shape[1]
    cumf = _cumsum_rows(jnp.pad(logf_all, ((0, 0), (0, 0), (0, LANES - nh_fox))))
    ka, vp = _kpack(k_all, v_all, cumf, nh_fox)

    h = _ffn(h, p['ffn1_g'][1], p['ffn1_w_in'][1], p['ffn1_w_out'][1])
    qa = _fox_q(h, p['mix_g'][1], p['fox_w_q'][0], cumf[:, n_keys - seq_len:], batch, seq_len, nh_fox)
    o = _fox_attn(qa, ka, vp, n_keys - seq_len)
    h = _proj_res(o, p['fox_w_out'][0], h)
    y = _ffn(h, p['ffn2_g'][1], p['ffn2_w_in'][1], p['ffn2_w_out'][1], final_g=p['final_g'])

    head_dim = width // nh_fox
    return (y.reshape(batch, seq_len, d), s_fin[None],
            k_new.reshape(batch, seq_len, nh_fox, head_dim), v_new.reshape(batch, seq_len, nh_fox, head_dim),
            logf_new)


def kernel(x_prompt, x_sample, state_ret, cache_k, cache_v, cache_logf, ffn1_g, ffn1_w_in, ffn1_w_out, mix_g,
           ffn2_g, ffn2_w_in, ffn2_w_out, ret_w_in, ret_gn_g, ret_w_out, kv_g, fox_w_kvf, fox_b_f, fox_w_q,
           fox_w_out, final_g):
    p = {'ffn1_g': ffn1_g, 'ffn1_w_in': ffn1_w_in, 'ffn1_w_out': ffn1_w_out, 'mix_g': mix_g,
         'ffn2_g': ffn2_g, 'ffn2_w_in': ffn2_w_in, 'ffn2_w_out': ffn2_w_out,
         'ret_w_in': ret_w_in, 'ret_gn_g': ret_gn_g, 'ret_w_out': ret_w_out,
         'kv_g': kv_g, 'fox_w_kvf': fox_w_kvf, 'fox_b_f': fox_b_f,
         'fox_w_q': fox_w_q, 'fox_w_out': fox_w_out, 'final_g': final_g}
    assert state_ret.shape[0] == 1 and fox_w_q.shape[0] == 1
    s0_prompt = jnp.zeros((1, x_prompt.shape[0]) + state_ret.shape[2:], F32)
    y_p, s_p, k_p, v_p, f_p = _trunk(x_prompt, 0, RET_CHUNK, s0_prompt, None, p)
    y_s, s_s, k_s, v_s, f_s = _trunk(x_sample, cache_k.shape[1], x_sample.shape[1], state_ret,
                                     (cache_k, cache_v, cache_logf), p)
    return (y_p, y_s, s_p, k_p, v_p, f_p, s_s, k_s, v_s, f_s)
```

```python
import functools

import jax
import jax.numpy as jnp
from jax import lax
from jax.experimental import pallas as pl
from jax.experimental.pallas import tpu as pltpu

F32 = jnp.float32
BF16 = jnp.bfloat16

EPS = 1e-6
ROPE_BASE = 10000.0
RET_CHUNK = 64
LANES = 128
HALF_LANES = LANES // 2
V7X_VMEM_LIMIT = 56 * 1024 * 1024

AUG_PIECE0 = 64
AUG_QPIECE0 = 112
N_PIECES = 3
VT_ROWS = 80


def _cparams(sem):
    return pltpu.CompilerParams(dimension_semantics=sem, vmem_limit_bytes=V7X_VMEM_LIMIT)


def _resident(shape):
    nd = len(shape)
    return pl.BlockSpec(shape, lambda *_: (0,) * nd, pipeline_mode=pl.Buffered(1))


def _rmsnorm(x, g):
    ms = jnp.mean(x * x, axis=-1, keepdims=True)
    return x * lax.rsqrt(ms + EPS) * g


def _silu(x):
    return x * jax.nn.sigmoid(x)


def _dot(a, b):
    return jnp.dot(a, b, preferred_element_type=F32)


def _dot_nt(a, b):
    return lax.dot_general(a, b, (((1,), (1,)), ((), ())), preferred_element_type=F32)


def _dot_tn(a, b):
    return lax.dot_general(a, b, (((0,), (0,)), ((), ())), preferred_element_type=F32)


def _split3(c):
    hi = c.astype(BF16).astype(F32)
    r = c - hi
    mid = r.astype(BF16).astype(F32)
    lo = (r - mid).astype(BF16).astype(F32)
    return hi, mid, lo


def _row_tile(n, want):
    t = min(n, want)
    assert n % t == 0 and t % 8 == 0, (n, t)
    return t


def _ffn_kernel(x_ref, g_ref, win_ref, wout_ref, fg_ref, o_ref, act_ref, *, d_ff, ck, final_norm):
    x = x_ref[...]
    xn = _rmsnorm(x, g_ref[...]).astype(BF16)
    for c in range(d_ff // ck):
        gate = _dot(xn, win_ref[:, c * ck:(c + 1) * ck])
        up = _dot(xn, win_ref[:, d_ff + c * ck:d_ff + (c + 1) * ck])
        act_ref[:, c * ck:(c + 1) * ck] = (_silu(gate) * up).astype(BF16)
    y = x + 0.5 * _dot(act_ref[...], wout_ref[...])
    if final_norm:
        y = _rmsnorm(y, fg_ref[...])
    o_ref[...] = y


def _ffn(h, g, w_in, w_out, final_g=None):
    n, d = h.shape
    d_ff = w_out.shape[0]
    tm = _row_tile(n, 512)
    ck = 256
    assert d_ff % ck == 0
    fg = jnp.ones((d,), F32) if final_g is None else final_g
    return pl.pallas_call(
        functools.partial(_ffn_kernel, d_ff=d_ff, ck=ck, final_norm=final_g is not None),
        out_shape=jax.ShapeDtypeStruct((n, d), F32),
        grid=(n // tm,),
        in_specs=[
            pl.BlockSpec((tm, d), lambda i: (i, 0)),
            _resident((1, d)),
            _resident((d, 2 * d_ff)),
            _resident((d_ff, d)),
            _resident((1, d)),
        ],
        out_specs=pl.BlockSpec((tm, d), lambda i: (i, 0)),
        scratch_shapes=[pltpu.VMEM((tm, d_ff), BF16)],
        compiler_params=_cparams(("parallel",)),
        name="ffn",
    )(h, g.reshape(1, d), w_in.astype(BF16), w_out.astype(BF16), fg.reshape(1, d))


def _proj_res_kernel(a_ref, w_ref, r_ref, o_ref):
    o_ref[...] = r_ref[...] + _dot(a_ref[...], w_ref[...])


def _proj_res(a, w, res):
    n, k = a.shape
    d = w.shape[1]
    tm = _row_tile(n, 512)
    return pl.pallas_call(
        _proj_res_kernel,
        out_shape=jax.ShapeDtypeStruct((n, d), F32),
        grid=(n // tm,),
        in_specs=[
            pl.BlockSpec((tm, k), lambda i: (i, 0)),
            _resident((k, d)),
            pl.BlockSpec((tm, d), lambda i: (i, 0)),
        ],
        out_specs=pl.BlockSpec((tm, d), lambda i: (i, 0)),
        compiler_params=_cparams(("parallel",)),
        name="proj_res",
    )(a, w.astype(BF16), res)


def _rope_tables(pos, half):
    inv = ROPE_BASE ** (-jnp.arange(half, dtype=F32) / half)
    ang = pos.astype(F32)[:, None] * inv[None, :]
    return jnp.cos(ang), jnp.sin(ang)


def _ret_proj_kernel(x_ref, g_ref, w_ref, cos_ref, sin_ref, q_ref, k_ref, v_ref, sg_ref, *, d, nh, rv):
    dk = d // nh
    half = dk // 2
    xn = _rmsnorm(x_ref[...], g_ref[...]).astype(BF16)
    cos = cos_ref[...]
    sin = sin_ref[...]
    for off, out_ref, scale in ((0, q_ref, dk ** -0.5), (d, k_ref, 1.0)):
        t = _dot(xn, w_ref[:, off:off + d])
        for hd in range(nh):
            x1 = t[:, hd * dk:hd * dk + half]
            x2 = t[:, hd * dk + half:(hd + 1) * dk]
            out_ref[:, hd * dk:hd * dk + half] = ((x1 * cos - x2 * sin) * scale).astype(BF16)
            out_ref[:, hd * dk + half:(hd + 1) * dk] = ((x1 * sin + x2 * cos) * scale).astype(BF16)
    v_ref[...] = _dot(xn, w_ref[:, 2 * d:2 * d + rv]).astype(BF16)
    sg_ref[...] = _silu(_dot(xn, w_ref[:, 2 * d + rv:2 * d + 2 * rv])).astype(BF16)


def _ret_proj(h, g, w_in, pos, seq_len, nh, rv):
    n, d = h.shape
    half = d // nh // 2
    tm = _row_tile(n, 512)
    cos, sin = _rope_tables(pos, half)
    if seq_len % tm == 0:
        period = seq_len // tm
    else:
        assert tm % seq_len == 0
        cos, sin = (jnp.tile(t, (tm // seq_len, 1)) for t in (cos, sin))
        period = 1
    tab = pl.BlockSpec((tm, half), lambda i: (i % period, 0))
    row = lambda w: pl.BlockSpec((tm, w), lambda i: (i, 0))
    return pl.pallas_call(
        functools.partial(_ret_proj_kernel, d=d, nh=nh, rv=rv),
        out_shape=(jax.ShapeDtypeStruct((n, d), BF16), jax.ShapeDtypeStruct((n, d), BF16),
                   jax.ShapeDtypeStruct((n, rv), BF16), jax.ShapeDtypeStruct((n, rv), BF16)),
        grid=(n // tm,),
        in_specs=[row(d), _resident((1, d)), _resident((d, 2 * d + 2 * rv)), tab, tab],
        out_specs=(row(d), row(d), row(rv), row(rv)),
        compiler_params=_cparams(("parallel",)),
        name="ret_proj",
    )(h, g.reshape(1, d), w_in.astype(BF16), cos, sin)


def _ret_decay_tables(nh, t, chunk):
    log_gamma = jnp.log1p(-jnp.exp2(-5.0 - jnp.arange(nh, dtype=F32)))
    pos = jnp.arange(t, dtype=F32)
    dist = jnp.abs(pos[:, None] - pos[None, :])
    cid = jnp.arange(t) // chunk
    visible = (cid[None, :] <= cid[:, None]).astype(F32)
    dmask = jnp.exp(log_gamma[:, None, None] * dist) * visible[None]
    qdec = jnp.exp(log_gamma[:, None] * (pos[None, :] + 1.0))[:, :, None]
    kdec = jnp.exp(log_gamma[:, None] * (t - 1.0 - pos[None, :]))[:, :, None]
    sdec = jnp.exp(log_gamma * t)
    return dmask, qdec, kdec, sdec


def _retention_kernel(sdec_ref, q_ref, k_ref, v_ref, sg_ref, dmask_ref, qdec_ref, kdec_ref, gn_ref, s0_ref,
                      o_ref, sout_ref, s_ref, *, nh, dk, dv):
    j = pl.program_id(1)

    @pl.when(j == 0)
    def _():
        s_ref[...] = s0_ref[...]

    for hd in range(nh):
        qh = q_ref[:, hd * dk:(hd + 1) * dk]
        kh = k_ref[:, hd * dk:(hd + 1) * dk]
        vh = v_ref[:, hd * dv:(hd + 1) * dv]
        state = s_ref[hd]
        scores = _dot_nt(qh, kh) * dmask_ref[hd]
        o = _dot(scores.astype(BF16), vh) + _dot(qh, state.astype(BF16)) * qdec_ref[hd]
        kd = (kh.astype(F32) * kdec_ref[hd]).astype(BF16)
        s_ref[hd] = sdec_ref[hd] * state + _dot_tn(kd, vh)
        mu = jnp.mean(o, axis=-1, keepdims=True)
        oc = o - mu
        var = jnp.mean(oc * oc, axis=-1, keepdims=True)
        on = oc * lax.rsqrt(var + EPS) * gn_ref[:, hd * dv:(hd + 1) * dv]
        o_ref[:, hd * dv:(hd + 1) * dv] = (sg_ref[:, hd * dv:(hd + 1) * dv].astype(F32) * on).astype(BF16)

    @pl.when(j == pl.num_programs(1) - 1)
    def _():
        sout_ref[...] = s_ref[...]


def _retention(q, k, v, sg, s0, gn_g, batch, seq_len, chunk):
    nh, dk, dv = s0.shape[1:]
    rv = nh * dv
    t = _row_tile(seq_len, max(chunk, 256))
    assert t % chunk == 0
    nblk = seq_len // t
    dmask, qdec, kdec, sdec = _ret_decay_tables(nh, t, chunk)
    row = lambda w: pl.BlockSpec((t, w), lambda b, j: (b * nblk + j, 0))
    st = pl.BlockSpec((None, nh, dk, dv), lambda b, j: (b, 0, 0, 0))
    return pl.pallas_call(
        functools.partial(_retention_kernel, nh=nh, dk=dk, dv=dv),
        out_shape=(jax.ShapeDtypeStruct((batch * seq_len, rv), BF16),
                   jax.ShapeDtypeStruct((batch, nh, dk, dv), F32)),
        grid=(batch, nblk),
        in_specs=[
            pl.BlockSpec(memory_space=pltpu.SMEM),
            row(nh * dk), row(nh * dk), row(rv), row(rv),
            _resident((nh, t, t)), _resident((nh, t, 1)), _resident((nh, t, 1)), _resident((1, rv)),
            st,
        ],
        out_specs=(row(rv), st),
        scratch_shapes=[pltpu.VMEM((nh, dk, dv), F32)],
        compiler_params=_cparams(("parallel", "arbitrary")),
        name="retention",
    )(sdec, q, k, v, sg, dmask, qdec, kdec, gn_g.reshape(1, rv), s0)


def _fox_kv_kernel(x_ref, g_ref, wk_ref, wv_ref, wf_ref, bf_ref, k_ref, v_ref, logf_ref):
    xn = _rmsnorm(x_ref[...], g_ref[...]).astype(BF16)
    k_ref[...] = _dot(xn, wk_ref[...])
    v_ref[...] = _dot(xn, wv_ref[...])
    z = _dot(xn, wf_ref[...]) + bf_ref[...]
    logf_ref[...] = jnp.minimum(z, 0.0) - jnp.log1p(jnp.exp(-jnp.abs(z)))


def _fox_kv(h, g, w_kvf, b_f, width, nh):
    n, d = h.shape
    tm = _row_tile(n, 512)
    wf = jnp.zeros((d, LANES), BF16).at[:, :nh].set(w_kvf[:, 2 * width:].astype(BF16))
    bf = jnp.zeros((1, LANES), F32).at[0, :nh].set(b_f)
    row = lambda w: pl.BlockSpec((tm, w), lambda i: (i, 0))
    return pl.pallas_call(
        _fox_kv_kernel,
        out_shape=(jax.ShapeDtypeStruct((n, width), F32), jax.ShapeDtypeStruct((n, width), F32),
                   jax.ShapeDtypeStruct((n, LANES), F32)),
        grid=(n // tm,),
        in_specs=[row(d), _resident((1, d)), _resident((d, width)), _resident((d, width)),
                  _resident((d, LANES)), _resident((1, LANES))],
        out_specs=(row(width), row(width), row(LANES)),
        compiler_params=_cparams(("parallel",)),
        name="fox_kv",
    )(h, g.reshape(1, d), w_kvf[:, :width].astype(BF16), w_kvf[:, width:2 * width].astype(BF16), wf, bf)


def _cumsum_kernel(x_ref, tri_ref, o_ref, *, tc, nblk):
    tri = tri_ref[...]

    def body(i, carry):
        r0 = pl.multiple_of(i * tc, tc)
        hi, mid, lo = _split3(x_ref[pl.ds(r0, tc), :])
        c = _dot(tri, hi.astype(BF16)) + _dot(tri, mid.astype(BF16)) + _dot(tri, lo.astype(BF16)) + carry
        o_ref[pl.ds(r0, tc), :] = c
        return c[tc - 1:tc, :]

    lax.fori_loop(0, nblk, body, jnp.zeros((1, x_ref.shape[-1]), F32))


def _cumsum_rows(x):
    b, n, w = x.shape
    tc = 64
    assert n % tc == 0
    tri = (jnp.arange(tc)[:, None] >= jnp.arange(tc)[None, :]).astype(BF16)
    blk = pl.BlockSpec((None, n, w), lambda i: (i, 0, 0))
    return pl.pallas_call(
        functools.partial(_cumsum_kernel, tc=tc, nblk=n // tc),
        out_shape=jax.ShapeDtypeStruct((b, n, w), F32),
        grid=(b,),
        in_specs=[blk, _resident((tc, tc))],
        out_specs=blk,
        compiler_params=_cparams(("parallel",)),
        name="cumsum",
    )(x, tri)


def _lane_iota(rows):
    return lax.broadcasted_iota(jnp.int32, (rows, LANES), 1)


def _head_lanes(x_ref, hd):
    t = x_ref[:, (hd // 2) * LANES:(hd // 2 + 1) * LANES]
    return pltpu.roll(t, HALF_LANES, axis=1) if hd % 2 else t


def _kpack_kernel(k_ref, v_ref, c_ref, ka_ref, vt_ref, *, nh, tk):
    rows = k_ref.shape[0]
    lane = _lane_iota(rows)
    hi, mid, lo = _split3(c_ref[...])
    extra = jnp.where(lane < AUG_QPIECE0 + N_PIECES, 1.0, 0.0)
    for p, piece in reversed(list(enumerate((hi, mid, lo)))):
        lo_lane = AUG_PIECE0 + p * nh
        extra = jnp.where(lane < lo_lane + nh, -pltpu.roll(piece, lo_lane, axis=1), extra)
    for hd in range(nh):
        ka_ref[hd] = jnp.where(lane < HALF_LANES, _head_lanes(k_ref, hd), extra).astype(BF16)
    vt = v_ref[...].T
    sub = lax.broadcasted_iota(jnp.int32, (VT_ROWS - HALF_LANES, rows), 0)
    ones_row = jnp.where(sub == 0, 1.0, 0.0)
    for hd in range(nh):
        t = jnp.concatenate([vt[hd * HALF_LANES:(hd + 1) * HALF_LANES, :], ones_row], axis=0).astype(BF16)
        for s in range(rows // tk):
            vt_ref[hd, s] = t[:, s * tk:(s + 1) * tk]


def _kpack(k, v, c, nh, tk):
    b, n, width = k.shape
    assert width == nh * HALF_LANES and N_PIECES * nh <= AUG_QPIECE0 - AUG_PIECE0
    assert nh & (nh - 1) == 0 and AUG_PIECE0 % nh == 0
    tm = max(tk, 512) if n % max(tk, 512) == 0 else tk
    assert n % tm == 0 and tm % tk == 0
    row = lambda w: pl.BlockSpec((None, tm, w), lambda i, j: (i, j, 0))
    return pl.pallas_call(
        functools.partial(_kpack_kernel, nh=nh, tk=tk),
        out_shape=(jax.ShapeDtypeStruct((b, nh, n, LANES), BF16),
                   jax.ShapeDtypeStruct((b, nh, n // tk, VT_ROWS, tk), BF16)),
        grid=(b, n // tm),
        in_specs=[row(width), row(width), row(LANES)],
        out_specs=(pl.BlockSpec((None, nh, tm, LANES), lambda i, j: (i, 0, j, 0)),
                   pl.BlockSpec((None, nh, tm // tk, VT_ROWS, tk), lambda i, j: (i, 0, j, 0, 0))),
        compiler_params=_cparams(("parallel", "parallel")),
        name="kpack",
    )(k, v, c)


def _fox_q_kernel(x_ref, g_ref, wt_ref, ct_ref, qa_ref, *, nh):
    tm = x_ref.shape[0]
    xn = _rmsnorm(x_ref[...], g_ref[...]).astype(BF16)
    qt = _dot_nt(wt_ref[...], xn) * (HALF_LANES ** -0.5)
    hi, mid, lo = _split3(ct_ref[...])
    sub = lax.broadcasted_iota(jnp.int32, (8, tm), 0)
    rowp = lax.broadcasted_iota(jnp.int32, (N_PIECES * nh, tm), 0)
    tail = jnp.zeros((LANES - AUG_QPIECE0 - 8, tm), F32)
    for hd in range(nh):
        onehot = jnp.where((rowp & (nh - 1)) == hd, 1.0, 0.0)
        own = [jnp.broadcast_to(a[hd:hd + 1, :], (8, tm)) for a in (hi, mid, lo)]
        pieces = jnp.where(sub == 0, own[0], jnp.where(sub == 1, own[1], jnp.where(sub == 2, own[2], 0.0)))
        tile = jnp.concatenate([qt[hd * HALF_LANES:(hd + 1) * HALF_LANES, :], onehot, pieces, tail], axis=0)
        qa_ref[hd] = tile.astype(BF16)


def _fox_q(h, g, w_q, cq_t, batch, seq_len, nh):
    n, d = h.shape
    tm = _row_tile(seq_len, 512)
    nblk = seq_len // tm
    return pl.pallas_call(
        functools.partial(_fox_q_kernel, nh=nh),
        out_shape=jax.ShapeDtypeStruct((batch, nh, LANES, seq_len), BF16),
        grid=(batch, nblk),
        in_specs=[
            pl.BlockSpec((tm, d), lambda b, j: (b * nblk + j, 0)),
            _resident((1, d)),
            _resident((nh * HALF_LANES, d)),
            pl.BlockSpec((None, nh, tm), lambda b, j: (b, 0, j)),
        ],
        out_specs=pl.BlockSpec((None, nh, LANES, tm), lambda b, j: (b, 0, 0, j)),
        compiler_params=_cparams(("parallel", "parallel")),
        name="fox_q",
    )(h, g.reshape(1, d), w_q.T.astype(BF16), cq_t)


def _fox_attn_kernel(q_ref, k_ref, v_ref, o_ref, s_ref, acc_ref, m_ref, *, hg, tq, tk, q_off, single_block):
    i = pl.program_id(2)
    q0 = q_off + i * tq
    n_full, n_mask = (0, 1) if single_block else (q0 // tk, tq // tk)
    kpos = lax.broadcasted_iota(jnp.int32, (tk, tq), 0)
    qpos = lax.broadcasted_iota(jnp.int32, (tk, tq), 1)
    acc_ref[...] = jnp.zeros_like(acc_ref)
    m_ref[...] = jnp.full_like(m_ref, -jnp.inf)

    def scores(j, slot):
        r0 = pl.multiple_of(j * tk, tk)
        for hh in range(hg):
            s_ref[slot, hh] = _dot(k_ref[hh, pl.ds(r0, tk), :], q_ref[hh])

    def absorb(j, slot, masked):
        for hh in range(hg):
            s = s_ref[slot, hh]
            if masked:
                s = jnp.where(q0 + qpos >= j * tk + kpos, s, -jnp.inf)
            m_prev = m_ref[hh]
            m_new = jnp.maximum(m_prev, jnp.max(s, axis=0, keepdims=True))
            alpha = jnp.exp(m_prev - m_new)
            p = jnp.exp(s - m_new).astype(BF16)
            acc_ref[hh] = alpha * acc_ref[hh] + _dot(v_ref[hh, j], p)
            m_ref[hh] = m_new

    scores(0, 0)

    def pair(jj, carry):
        j = 2 * jj
        absorb(j, 0, False)
        scores(j + 1, 1)
        absorb(j + 1, 1, False)
        scores(j + 2, 0)
        return carry

    lax.fori_loop(0, n_full // 2, pair, 0)
    for t in range(n_mask):
        absorb(n_full + t, t % 2, True)
        if t + 1 < n_mask:
            scores(n_full + t + 1, (t + 1) % 2)
    for g in range(hg // 2):
        halves = []
        for hh in (2 * g, 2 * g + 1):
            a = acc_ref[hh]
            halves.append(a[:HALF_LANES, :] / a[HALF_LANES:HALF_LANES + 1, :])
        o_ref[:, g * LANES:(g + 1) * LANES] = jnp.concatenate(halves, axis=0).T.astype(BF16)


def _fox_attn(qa, ka, vt, q_off):
    b, nh, _, seq_len = qa.shape
    n_keys = ka.shape[2]
    tk = vt.shape[-1]
    hg = 4
    tq = min(seq_len, 512)
    assert seq_len % tq == 0 and tq % LANES == 0 and nh % hg == 0 and q_off + seq_len <= n_keys
    nq = seq_len // tq
    single_block = n_keys == tk
    assert (single_block and nq == 1) or (tq % (2 * tk) == 0 and q_off % (2 * tk) == 0)
    return pl.pallas_call(
        functools.partial(_fox_attn_kernel, hg=hg, tq=tq, tk=tk, q_off=q_off, single_block=single_block),
        out_shape=jax.ShapeDtypeStruct((b * seq_len, nh * HALF_LANES), BF16),
        grid=(b, nh // hg, nq),
        in_specs=[
            pl.BlockSpec((None, hg, LANES, tq), lambda bi, g, i: (bi, g, 0, i)),
            pl.BlockSpec((None, hg, n_keys, LANES), lambda bi, g, i: (bi, g, 0, 0)),
            pl.BlockSpec((None, hg, n_keys // tk, VT_ROWS, tk), lambda bi, g, i: (bi, g, 0, 0, 0)),
        ],
        out_specs=pl.BlockSpec((tq, hg * HALF_LANES), lambda bi, g, i: (bi * nq + i, g)),
        scratch_shapes=[pltpu.VMEM((2, hg, tk, tq), F32), pltpu.VMEM((hg, VT_ROWS, tq), F32),
                        pltpu.VMEM((hg, 1, tq), F32)],
        compiler_params=_cparams(("parallel", "parallel", "arbitrary")),
        name="fox_attn",
    )(qa, ka, vt)


def _trunk(x, pos0, chunk, s0, past, p):
    batch, seq_len, d = x.shape
    nh_ret, dk, dv = s0.shape[2:]
    rv = nh_ret * dv
    nh_fox = p['fox_b_f'].shape[0]
    width = p['fox_w_q'].shape[-1]
    pos = pos0 + jnp.arange(seq_len)
    h = x.reshape(batch * seq_len, d)

    h = _ffn(h, p['ffn1_g'][0], p['ffn1_w_in'][0], p['ffn1_w_out'][0])
    q, k, v, sg = _ret_proj(h, p['mix_g'][0], p['ret_w_in'][0], pos, seq_len, nh_ret, rv)
    o, s_fin = _retention(q, k, v, sg, s0[0], p['ret_gn_g'][0], batch, seq_len, chunk)
    h = _proj_res(o, p['ret_w_out'][0], h)
    h = _ffn(h, p['ffn2_g'][0], p['ffn2_w_in'][0], p['ffn2_w_out'][0])

    k_new, v_new, logf_pad = _fox_kv(h, p['kv_g'], p['fox_w_kvf'], p['fox_b_f'], width, nh_fox)
    k_new = k_new.reshape(batch, seq_len, width)
    v_new = v_new.reshape(batch, seq_len, width)
    logf_new = logf_pad.reshape(batch, seq_len, LANES)[:, :, :nh_fox]
    if past is None:
        k_all, v_all, logf_all = k_new, v_new, logf_new
    else:
        past_len = past[0].shape[1]
        k_all = jnp.concatenate([past[0].reshape(batch, past_len, width), k_new], axis=1)
        v_all = jnp.concatenate([past[1].reshape(batch, past_len, width), v_new], axis=1)
        logf_all = jnp.concatenate([past[2], logf_new], axis=1)
    n_keys = k_all.shape[1]
    q_off = n_keys - seq_len
    pad_k = -n_keys % LANES
    pad_q = -seq_len % LANES
    assert pad_q <= pad_k
    if pad_k:
        k_all, v_all, logf_all = (jnp.pad(t, ((0, 0), (0, pad_k), (0, 0))) for t in (k_all, v_all, logf_all))
    n_pad = n_keys + pad_k
    tk = 256 if n_pad % 256 == 0 else n_pad
    cumf = _cumsum_rows(jnp.pad(logf_all, ((0, 0), (0, 0), (0, LANES - nh_fox))))
    ka, vt = _kpack(k_all, v_all, cumf, nh_fox, tk)

    h = _ffn(h, p['ffn1_g'][1], p['ffn1_w_in'][1], p['ffn1_w_out'][1])
    cq_t = jnp.swapaxes(cumf[:, q_off:q_off + seq_len, :nh_fox], 1, 2)
    qa = _fox_q(h, p['mix_g'][1], p['fox_w_q'][0], cq_t, batch, seq_len, nh_fox)
    if pad_q:
        qa = jnp.pad(qa, ((0, 0), (0, 0), (0, 0), (0, pad_q)))
    o = _fox_attn(qa, ka, vt, q_off)
    if pad_q:
        o = o.reshape(batch, seq_len + pad_q, width)[:, :seq_len].reshape(batch * seq_len, width)
    h = _proj_res(o, p['fox_w_out'][0], h)
    y = _ffn(h, p['ffn2_g'][1], p['ffn2_w_in'][1], p['ffn2_w_out'][1], final_g=p['final_g'])

    head_dim = width // nh_fox
    return (y.reshape(batch, seq_len, d), s_fin[None],
            k_new.reshape(batch, seq_len, nh_fox, head_dim), v_new.reshape(batch, seq_len, nh_fox, head_dim),
            logf_new)


def kernel(x_prompt, x_sample, state_ret, cache_k, cache_v, cache_logf, ffn1_g, ffn1_w_in, ffn1_w_out, mix_g,
           ffn2_g, ffn2_w_in, ffn2_w_out, ret_w_in, ret_gn_g, ret_w_out, kv_g, fox_w_kvf, fox_b_f, fox_w_q,
           fox_w_out, final_g):
    p = {'ffn1_g': ffn1_g, 'ffn1_w_in': ffn1_w_in, 'ffn1_w_out': ffn1_w_out, 'mix_g': mix_g,
         'ffn2_g': ffn2_g, 'ffn2_w_in': ffn2_w_in, 'ffn2_w_out': ffn2_w_out,
         'ret_w_in': ret_w_in, 'ret_gn_g': ret_gn_g, 'ret_w_out': ret_w_out,
         'kv_g': kv_g, 'fox_w_kvf': fox_w_kvf, 'fox_b_f': fox_b_f,
         'fox_w_q': fox_w_q, 'fox_w_out': fox_w_out, 'final_g': final_g}
    assert state_ret.shape[0] == 1 and fox_w_q.shape[0] == 1
    s0_prompt = jnp.zeros((1, x_prompt.shape[0]) + state_ret.shape[2:], F32)
    y_p, s_p, k_p, v_p, f_p = _trunk(x_prompt, 0, RET_CHUNK, s0_prompt, None, p)
    y_s, s_s, k_s, v_s, f_s = _trunk(x_sample, cache_k.shape[1], x_sample.shape[1], state_ret,
                                     (cache_k, cache_v, cache_logf), p)
    return (y_p, y_s, s_p, k_p, v_p, f_p, s_s, k_s, v_s, f_s)
```

```python
import functools

import jax
import jax.numpy as jnp
from jax import lax
from jax.experimental import pallas as pl
from jax.experimental.pallas import tpu as pltpu

F32 = jnp.float32
BF16 = jnp.bfloat16

EPS = 1e-6
ROPE_BASE = 10000.0
RET_CHUNK = 64
LANES = 128
HALF_LANES = LANES // 2
V7X_VMEM_LIMIT = 56 * 1024 * 1024

AUG_PIECE0 = 64
AUG_QPIECE0 = 112
N_PIECES = 3
VT_ROWS = 80
LOG2E = 1.4426950408889634


def _cparams(sem, flags=None):
    return pltpu.CompilerParams(dimension_semantics=sem, vmem_limit_bytes=V7X_VMEM_LIMIT, flags=flags)


def _resident(shape):
    nd = len(shape)
    return pl.BlockSpec(shape, lambda *_: (0,) * nd, pipeline_mode=pl.Buffered(1))


def _rmsnorm(x, g):
    ms = jnp.mean(x * x, axis=-1, keepdims=True)
    return x * lax.rsqrt(ms + EPS) * g


def _silu(x):
    return x * jax.nn.sigmoid(x)


def _dot(a, b):
    return jnp.dot(a, b, preferred_element_type=F32)


def _dot_nt(a, b):
    return lax.dot_general(a, b, (((1,), (1,)), ((), ())), preferred_element_type=F32)


def _dot_tn(a, b):
    return lax.dot_general(a, b, (((0,), (0,)), ((), ())), preferred_element_type=F32)


def _split3(c):
    hi = c.astype(BF16).astype(F32)
    r = c - hi
    mid = r.astype(BF16).astype(F32)
    lo = (r - mid).astype(BF16).astype(F32)
    return hi, mid, lo


def _row_tile(n, want):
    t = min(n, want)
    assert n % t == 0 and t % 8 == 0, (n, t)
    return t


def _ffn_kernel(x_ref, g_ref, win_ref, wout_ref, fg_ref, o_ref, act_ref, *, d_ff, ck, final_norm):
    x = x_ref[...]
    xn = _rmsnorm(x, g_ref[...]).astype(BF16)
    for c in range(d_ff // ck):
        gate = _dot(xn, win_ref[:, c * ck:(c + 1) * ck])
        up = _dot(xn, win_ref[:, d_ff + c * ck:d_ff + (c + 1) * ck])
        act_ref[:, c * ck:(c + 1) * ck] = (_silu(gate) * up).astype(BF16)
    y = x + 0.5 * _dot(act_ref[...], wout_ref[...])
    if final_norm:
        y = _rmsnorm(y, fg_ref[...])
    o_ref[...] = y


def _ffn(h, g, w_in, w_out, final_g=None):
    n, d = h.shape
    d_ff = w_out.shape[0]
    tm = _row_tile(n, 512)
    ck = 256
    assert d_ff % ck == 0
    fg = jnp.ones((d,), F32) if final_g is None else final_g
    return pl.pallas_call(
        functools.partial(_ffn_kernel, d_ff=d_ff, ck=ck, final_norm=final_g is not None),
        out_shape=jax.ShapeDtypeStruct((n, d), F32),
        grid=(n // tm,),
        in_specs=[
            pl.BlockSpec((tm, d), lambda i: (i, 0)),
            _resident((1, d)),
            _resident((d, 2 * d_ff)),
            _resident((d_ff, d)),
            _resident((1, d)),
        ],
        out_specs=pl.BlockSpec((tm, d), lambda i: (i, 0)),
        scratch_shapes=[pltpu.VMEM((tm, d_ff), BF16)],
        compiler_params=_cparams(("parallel",)),
        name="ffn",
    )(h, g.reshape(1, d), w_in.astype(BF16), w_out.astype(BF16), fg.reshape(1, d))


def _proj_res_kernel(a_ref, w_ref, r_ref, o_ref):
    o_ref[...] = r_ref[...] + _dot(a_ref[...], w_ref[...])


def _proj_res(a, w, res):
    n, k = a.shape
    d = w.shape[1]
    tm = _row_tile(n, 512)
    return pl.pallas_call(
        _proj_res_kernel,
        out_shape=jax.ShapeDtypeStruct((n, d), F32),
        grid=(n // tm,),
        in_specs=[
            pl.BlockSpec((tm, k), lambda i: (i, 0)),
            _resident((k, d)),
            pl.BlockSpec((tm, d), lambda i: (i, 0)),
        ],
        out_specs=pl.BlockSpec((tm, d), lambda i: (i, 0)),
        compiler_params=_cparams(("parallel",)),
        name="proj_res",
    )(a, w.astype(BF16), res)


def _rope_tables(pos, half):
    inv = ROPE_BASE ** (-jnp.arange(half, dtype=F32) / half)
    ang = pos.astype(F32)[:, None] * inv[None, :]
    return jnp.cos(ang), jnp.sin(ang)


def _ret_proj_kernel(x_ref, g_ref, w_ref, cos_ref, sin_ref, q_ref, k_ref, v_ref, sg_ref, *, d, nh, rv):
    dk = d // nh
    half = dk // 2
    xn = _rmsnorm(x_ref[...], g_ref[...]).astype(BF16)
    cos = cos_ref[...]
    sin = sin_ref[...]
    for off, out_ref, scale in ((0, q_ref, dk ** -0.5), (d, k_ref, 1.0)):
        t = _dot(xn, w_ref[:, off:off + d])
        for hd in range(nh):
            x1 = t[:, hd * dk:hd * dk + half]
            x2 = t[:, hd * dk + half:(hd + 1) * dk]
            out_ref[:, hd * dk:hd * dk + half] = ((x1 * cos - x2 * sin) * scale).astype(BF16)
            out_ref[:, hd * dk + half:(hd + 1) * dk] = ((x1 * sin + x2 * cos) * scale).astype(BF16)
    v_ref[...] = _dot(xn, w_ref[:, 2 * d:2 * d + rv]).astype(BF16)
    sg_ref[...] = _silu(_dot(xn, w_ref[:, 2 * d + rv:2 * d + 2 * rv])).astype(BF16)


def _ret_proj(h, g, w_in, pos, seq_len, nh, rv):
    n, d = h.shape
    half = d // nh // 2
    tm = _row_tile(n, 512)
    cos, sin = _rope_tables(pos, half)
    if seq_len % tm == 0:
        period = seq_len // tm
    else:
        assert tm % seq_len == 0
        cos, sin = (jnp.tile(t, (tm // seq_len, 1)) for t in (cos, sin))
        period = 1
    tab = pl.BlockSpec((tm, half), lambda i: (i % period, 0))
    row = lambda w: pl.BlockSpec((tm, w), lambda i: (i, 0))
    return pl.pallas_call(
        functools.partial(_ret_proj_kernel, d=d, nh=nh, rv=rv),
        out_shape=(jax.ShapeDtypeStruct((n, d), BF16), jax.ShapeDtypeStruct((n, d), BF16),
                   jax.ShapeDtypeStruct((n, rv), BF16), jax.ShapeDtypeStruct((n, rv), BF16)),
        grid=(n // tm,),
        in_specs=[row(d), _resident((1, d)), _resident((d, 2 * d + 2 * rv)), tab, tab],
        out_specs=(row(d), row(d), row(rv), row(rv)),
        compiler_params=_cparams(("parallel",)),
        name="ret_proj",
    )(h, g.reshape(1, d), w_in.astype(BF16), cos, sin)


def _ret_decay_tables(nh, t, chunk):
    log_gamma = jnp.log1p(-jnp.exp2(-5.0 - jnp.arange(nh, dtype=F32)))
    pos = jnp.arange(t, dtype=F32)
    dist = jnp.abs(pos[:, None] - pos[None, :])
    cid = jnp.arange(t) // chunk
    visible = (cid[None, :] <= cid[:, None]).astype(F32)
    dmask = jnp.exp(log_gamma[:, None, None] * dist) * visible[None]
    qdec = jnp.exp(log_gamma[:, None] * (pos[None, :] + 1.0))[:, :, None]
    kdec = jnp.exp(log_gamma[:, None] * (t - 1.0 - pos[None, :]))[:, :, None]
    sdec = jnp.exp(log_gamma * t)
    return dmask, qdec, kdec, sdec


def _retention_kernel(sdec_ref, q_ref, k_ref, v_ref, sg_ref, dmask_ref, qdec_ref, kdec_ref, gn_ref, s0_ref,
                      o_ref, sout_ref, s_ref, *, nh, dk, dv):
    j = pl.program_id(1)

    @pl.when(j == 0)
    def _():
        s_ref[...] = s0_ref[...]

    for hd in range(nh):
        qh = q_ref[:, hd * dk:(hd + 1) * dk]
        kh = k_ref[:, hd * dk:(hd + 1) * dk]
        vh = v_ref[:, hd * dv:(hd + 1) * dv]
        state = s_ref[hd]
        scores = _dot_nt(qh, kh) * dmask_ref[hd]
        o = _dot(scores.astype(BF16), vh) + _dot(qh, state.astype(BF16)) * qdec_ref[hd]
        kd = (kh.astype(F32) * kdec_ref[hd]).astype(BF16)
        s_ref[hd] = sdec_ref[hd] * state + _dot_tn(kd, vh)
        mu = jnp.mean(o, axis=-1, keepdims=True)
        oc = o - mu
        var = jnp.mean(oc * oc, axis=-1, keepdims=True)
        on = oc * lax.rsqrt(var + EPS) * gn_ref[:, hd * dv:(hd + 1) * dv]
        o_ref[:, hd * dv:(hd + 1) * dv] = (sg_ref[:, hd * dv:(hd + 1) * dv].astype(F32) * on).astype(BF16)

    @pl.when(j == pl.num_programs(1) - 1)
    def _():
        sout_ref[...] = s_ref[...]


def _retention(q, k, v, sg, s0, gn_g, batch, seq_len, chunk):
    nh, dk, dv = s0.shape[1:]
    rv = nh * dv
    t = _row_tile(seq_len, max(chunk, 256))
    assert t % chunk == 0
    nblk = seq_len // t
    dmask, qdec, kdec, sdec = _ret_decay_tables(nh, t, chunk)
    row = lambda w: pl.BlockSpec((t, w), lambda b, j: (b * nblk + j, 0))
    st = pl.BlockSpec((None, nh, dk, dv), lambda b, j: (b, 0, 0, 0))
    return pl.pallas_call(
        functools.partial(_retention_kernel, nh=nh, dk=dk, dv=dv),
        out_shape=(jax.ShapeDtypeStruct((batch * seq_len, rv), BF16),
                   jax.ShapeDtypeStruct((batch, nh, dk, dv), F32)),
        grid=(batch, nblk),
        in_specs=[
            pl.BlockSpec(memory_space=pltpu.SMEM),
            row(nh * dk), row(nh * dk), row(rv), row(rv),
            _resident((nh, t, t)), _resident((nh, t, 1)), _resident((nh, t, 1)), _resident((1, rv)),
            st,
        ],
        out_specs=(row(rv), st),
        scratch_shapes=[pltpu.VMEM((nh, dk, dv), F32)],
        compiler_params=_cparams(("parallel", "arbitrary")),
        name="retention",
    )(sdec, q, k, v, sg, dmask, qdec, kdec, gn_g.reshape(1, rv), s0)


def _lane_iota(rows):
    return lax.broadcasted_iota(jnp.int32, (rows, LANES), 1)


def _head_lanes(x, hd):
    t = x[:, (hd // 2) * LANES:(hd // 2 + 1) * LANES]
    return pltpu.roll(t, HALF_LANES, axis=1) if hd % 2 else t


def _tri_cumsum(tri, x):
    hi, mid, lo = _split3(x)
    return _dot(tri, hi.astype(BF16)) + _dot(tri, mid.astype(BF16)) + _dot(tri, lo.astype(BF16))


def _pack_keys(k, c, ka_ref, nh):
    rows = k.shape[0]
    lane = _lane_iota(rows)
    hi, mid, lo = _split3(c * LOG2E)
    extra = jnp.where(lane < AUG_QPIECE0 + N_PIECES, 1.0, 0.0)
    for p, piece in reversed(list(enumerate((hi, mid, lo)))):
        lo_lane = AUG_PIECE0 + p * nh
        extra = jnp.where(lane < lo_lane + nh, -pltpu.roll(piece, lo_lane, axis=1), extra)
    for hd in range(nh):
        ka_ref[hd] = jnp.where(lane < HALF_LANES, _head_lanes(k, hd), extra).astype(BF16)


def _pack_values(v, vt_ref, nh, tk):
    rows = v.shape[0]
    vt = v.T
    sub = lax.broadcasted_iota(jnp.int32, (VT_ROWS - HALF_LANES, rows), 0)
    ones_row = jnp.where(sub == 0, 1.0, 0.0)
    for hd in range(nh):
        t = jnp.concatenate([vt[hd * HALF_LANES:(hd + 1) * HALF_LANES, :], ones_row], axis=0).astype(BF16)
        for s in range(rows // tk):
            vt_ref[hd, s] = t[:, s * tk:(s + 1) * tk]


def _fox_kv_kernel(x_ref, g_ref, wk_ref, wv_ref, wf_ref, bf_ref, tri_ref, k_ref, v_ref, logf_ref, *rest,
                   nh, tk, packed):
    tm = x_ref.shape[0]
    xn = _rmsnorm(x_ref[...], g_ref[...]).astype(BF16)
    k = _dot(xn, wk_ref[...])
    v = _dot(xn, wv_ref[...])
    z = _dot(xn, wf_ref[...]) + bf_ref[...]
    logf = jnp.minimum(z, 0.0) - jnp.log1p(jnp.exp(-jnp.abs(z)))
    logf_ref[...] = logf
    for src, dst in ((k, k_ref), (v, v_ref)):
        for hd in range(nh):
            dst[pl.ds(hd, tm, stride=nh), :] = _head_lanes(src, hd)[:, :HALF_LANES]
    if packed:
        ka_ref, vt_ref, ct_ref, carry_ref = rest

        @pl.when(pl.program_id(1) == 0)
        def _():
            carry_ref[...] = jnp.zeros_like(carry_ref)

        c = _tri_cumsum(tri_ref[...], jnp.where(_lane_iota(tm) < nh, logf, 0.0)) + carry_ref[...]
        carry_ref[...] = c[tm - 1:tm, :]
        ct_ref[...] = c.T[:nh, :]
        _pack_keys(k, c, ka_ref, nh)
        _pack_values(v, vt_ref, nh, tk)


def _fox_kv(h, g, w_kvf, b_f, width, nh, batch, seq_len, tk=None):
    n, d = h.shape
    packed = tk is not None
    tm = _row_tile(seq_len if packed else n, 512)
    grid = (batch, seq_len // tm) if packed else (n // tm, 1)
    nblk = grid[1]
    assert width == nh * HALF_LANES and N_PIECES * nh <= AUG_QPIECE0 - AUG_PIECE0
    assert nh & (nh - 1) == 0 and AUG_PIECE0 % nh == 0
    wf = jnp.zeros((d, LANES), BF16).at[:, :nh].set(w_kvf[:, 2 * width:].astype(BF16))
    bf = jnp.zeros((1, LANES), F32).at[0, :nh].set(b_f)
    tri = (jnp.arange(tm)[:, None] >= jnp.arange(tm)[None, :]).astype(BF16)
    row = lambda r, w: pl.BlockSpec((r, w), lambda i, j: (i * nblk + j, 0))
    out_shape = [jax.ShapeDtypeStruct((n * nh, HALF_LANES), F32), jax.ShapeDtypeStruct((n * nh, HALF_LANES), F32),
                 jax.ShapeDtypeStruct((n, LANES), F32)]
    out_specs = [row(tm * nh, HALF_LANES), row(tm * nh, HALF_LANES), row(tm, LANES)]
    scratch = []
    if packed:
        assert tm % tk == 0
        out_shape += [jax.ShapeDtypeStruct((batch, nh, seq_len, LANES), BF16),
                      jax.ShapeDtypeStruct((batch, nh, seq_len // tk, VT_ROWS, tk), BF16),
                      jax.ShapeDtypeStruct((batch, nh, seq_len), F32)]
        out_specs += [pl.BlockSpec((None, nh, tm, LANES), lambda i, j: (i, 0, j, 0)),
                      pl.BlockSpec((None, nh, tm // tk, VT_ROWS, tk), lambda i, j: (i, 0, j, 0, 0)),
                      pl.BlockSpec((None, nh, tm), lambda i, j: (i, 0, j))]
        scratch = [pltpu.VMEM((1, LANES), F32)]
    return pl.pallas_call(
        functools.partial(_fox_kv_kernel, nh=nh, tk=tk, packed=packed),
        out_shape=tuple(out_shape),
        grid=grid,
        in_specs=[row(tm, d), _resident((1, d)), _resident((d, width)), _resident((d, width)),
                  _resident((d, LANES)), _resident((1, LANES)), _resident((tm, tm))],
        out_specs=tuple(out_specs),
        scratch_shapes=scratch,
        compiler_params=_cparams(("parallel", "arbitrary")),
        name="fox_kv",
    )(h, g.reshape(1, d), w_kvf[:, :width].astype(BF16), w_kvf[:, width:2 * width].astype(BF16), wf, bf, tri)


def _cumsum_kernel(x_ref, tri_ref, o_ref, *, tc, nblk):
    tri = tri_ref[...]

    def body(i, carry):
        r0 = pl.multiple_of(i * tc, tc)
        c = _tri_cumsum(tri, x_ref[pl.ds(r0, tc), :]) + carry
        o_ref[pl.ds(r0, tc), :] = c
        return c[tc - 1:tc, :]

    lax.fori_loop(0, nblk, body, jnp.zeros((1, x_ref.shape[-1]), F32))


def _cumsum_rows(x):
    b, n, w = x.shape
    tc = 128 if n % 128 == 0 else 64
    assert n % tc == 0
    tri = (jnp.arange(tc)[:, None] >= jnp.arange(tc)[None, :]).astype(BF16)
    blk = pl.BlockSpec((None, n, w), lambda i: (i, 0, 0))
    return pl.pallas_call(
        functools.partial(_cumsum_kernel, tc=tc, nblk=n // tc),
        out_shape=jax.ShapeDtypeStruct((b, n, w), F32),
        grid=(b,),
        in_specs=[blk, _resident((tc, tc))],
        out_specs=blk,
        compiler_params=_cparams(("parallel",)),
        name="cumsum",
    )(x, tri)


def _kpack_kernel(k_ref, v_ref, c_ref, ka_ref, vt_ref, *, nh, tk):
    _pack_keys(k_ref[...], c_ref[...], ka_ref, nh)
    _pack_values(v_ref[...], vt_ref, nh, tk)


def _kpack(k, v, c, nh, tk):
    b, n, width = k.shape
    tm = max(tk, 512) if n % max(tk, 512) == 0 else tk
    assert n % tm == 0 and tm % tk == 0
    row = lambda w: pl.BlockSpec((None, tm, w), lambda i, j: (i, j, 0))
    return pl.pallas_call(
        functools.partial(_kpack_kernel, nh=nh, tk=tk),
        out_shape=(jax.ShapeDtypeStruct((b, nh, n, LANES), BF16),
                   jax.ShapeDtypeStruct((b, nh, n // tk, VT_ROWS, tk), BF16)),
        grid=(b, n // tm),
        in_specs=[row(width), row(width), row(LANES)],
        out_specs=(pl.BlockSpec((None, nh, tm, LANES), lambda i, j: (i, 0, j, 0)),
                   pl.BlockSpec((None, nh, tm // tk, VT_ROWS, tk), lambda i, j: (i, 0, j, 0, 0))),
        compiler_params=_cparams(("parallel", "parallel")),
        name="kpack",
    )(k, v, c)


def _fox_q_kernel(x_ref, g_ref, wt_ref, ct_ref, qa_ref, *, nh):
    tm = x_ref.shape[0]
    xn = _rmsnorm(x_ref[...], g_ref[...]).astype(BF16)
    qt = _dot_nt(wt_ref[...], xn) * (HALF_LANES ** -0.5 * LOG2E)
    hi, mid, lo = _split3(ct_ref[...] * LOG2E)
    sub = lax.broadcasted_iota(jnp.int32, (8, tm), 0)
    rowp = lax.broadcasted_iota(jnp.int32, (N_PIECES * nh, tm), 0)
    tail = jnp.zeros((LANES - AUG_QPIECE0 - 8, tm), F32)
    for hd in range(nh):
        onehot = jnp.where((rowp & (nh - 1)) == hd, 1.0, 0.0)
        own = [jnp.broadcast_to(a[hd:hd + 1, :], (8, tm)) for a in (hi, mid, lo)]
        pieces = jnp.where(sub == 0, own[0], jnp.where(sub == 1, own[1], jnp.where(sub == 2, own[2], 0.0)))
        tile = jnp.concatenate([qt[hd * HALF_LANES:(hd + 1) * HALF_LANES, :], onehot, pieces, tail], axis=0)
        qa_ref[hd] = tile.astype(BF16)


def _fox_q(h, g, w_q, cq_t, batch, seq_len, nh):
    n, d = h.shape
    tm = _row_tile(seq_len, 512)
    nblk = seq_len // tm
    return pl.pallas_call(
        functools.partial(_fox_q_kernel, nh=nh),
        out_shape=jax.ShapeDtypeStruct((batch, nh, LANES, seq_len), BF16),
        grid=(batch, nblk),
        in_specs=[
            pl.BlockSpec((tm, d), lambda b, j: (b * nblk + j, 0)),
            _resident((1, d)),
            _resident((nh * HALF_LANES, d)),
            pl.BlockSpec((None, nh, tm), lambda b, j: (b, 0, j)),
        ],
        out_specs=pl.BlockSpec((None, nh, LANES, tm), lambda b, j: (b, 0, 0, j)),
        compiler_params=_cparams(("parallel", "parallel")),
        name="fox_q",
    )(h, g.reshape(1, d), w_q.T.astype(BF16), cq_t)


def _fox_attn_kernel(q_ref, k_ref, v_ref, o_ref, s0_ref, s1_ref, acc_ref, m_ref, *, hg, tq, tk, q_off,
                     single_block):
    i = pl.program_id(2)
    q0 = q_off + i * tq
    n_full, n_mask = (0, 1) if single_block else (q0 // tk, tq // tk)
    acc_ref[...] = jnp.zeros_like(acc_ref)
    m_ref[...] = jnp.full_like(m_ref, -jnp.inf)
    slots = (s0_ref, s1_ref)

    def scores(j, slot, c0=0):
        r0 = pl.multiple_of(j * tk, tk)
        for hh in range(hg):
            slots[slot][hh, :, c0:] = _dot(k_ref[hh, pl.ds(r0, tk), :], q_ref[hh, :, c0:])

    def absorb(j, slot, c0=0, c1=tq, diag=None):
        w = c1 - c0
        for hh in range(hg):
            s = slots[slot][hh, :, c0:c1]
            if diag is not None:
                kpos = lax.broadcasted_iota(jnp.int32, (tk, w), 0)
                qpos = lax.broadcasted_iota(jnp.int32, (tk, w), 1)
                s = jnp.where(qpos + diag >= kpos, s, -jnp.inf)
            m_prev = m_ref[hh, :, c0:c1]
            m_new = jnp.maximum(m_prev, jnp.max(s, axis=0, keepdims=True))
            alpha = jnp.exp2(m_prev - m_new)
            p = jnp.exp2(s - m_new).astype(BF16)
            acc_ref[hh, :, c0:c1] = alpha * acc_ref[hh, :, c0:c1] + _dot(v_ref[hh, j], p)
            m_ref[hh, :, c0:c1] = m_new

    scores(0, 0)

    def pair(jj, carry):
        j = 2 * jj
        scores(j + 1, 1)
        absorb(j, 0)
        scores(j + 2, 0)
        absorb(j + 1, 1)
        return carry

    lax.fori_loop(0, n_full // 2, pair, 0)
    for t in range(n_mask):
        lo = 0 if single_block else t * tk
        hi = min(lo + tk, tq)
        if t + 1 < n_mask:
            scores(n_full + t + 1, (t + 1) % 2, hi)
        absorb(n_full + t, t % 2, lo, hi, diag=q_off if single_block else 0)
        if hi < tq:
            absorb(n_full + t, t % 2, hi, tq)
    for g in range(hg // 2):
        halves = []
        for hh in (2 * g, 2 * g + 1):
            a = acc_ref[hh]
            halves.append(a[:HALF_LANES, :] / a[HALF_LANES:HALF_LANES + 1, :])
        o_ref[:, g * LANES:(g + 1) * LANES] = jnp.concatenate(halves, axis=0).T.astype(BF16)


def _fox_attn(qa, ka, vt, q_off):
    b, nh, _, seq_len = qa.shape
    n_keys = ka.shape[2]
    tk = vt.shape[-1]
    hg = 8
    tq = min(seq_len, 512)
    assert seq_len % tq == 0 and tq % LANES == 0 and nh % hg == 0 and q_off + seq_len <= n_keys
    nq = seq_len // tq
    single_block = n_keys == tk
    assert (single_block and nq == 1) or (tq % (2 * tk) == 0 and q_off % (2 * tk) == 0)
    return pl.pallas_call(
        functools.partial(_fox_attn_kernel, hg=hg, tq=tq, tk=tk, q_off=q_off, single_block=single_block),
        out_shape=jax.ShapeDtypeStruct((b * seq_len, nh * HALF_LANES), BF16),
        grid=(b, nh // hg, nq),
        in_specs=[
            pl.BlockSpec((None, hg, LANES, tq), lambda bi, g, i: (bi, g, 0, i)),
            pl.BlockSpec((None, hg, n_keys, LANES), lambda bi, g, i: (bi, g, 0, 0)),
            pl.BlockSpec((None, hg, n_keys // tk, VT_ROWS, tk), lambda bi, g, i: (bi, g, 0, 0, 0)),
        ],
        out_specs=pl.BlockSpec((tq, hg * HALF_LANES), lambda bi, g, i: (bi * nq + i, g)),
        scratch_shapes=[pltpu.VMEM((hg, tk, tq), F32), pltpu.VMEM((hg, tk, tq), F32),
                        pltpu.VMEM((hg, VT_ROWS, tq), F32), pltpu.VMEM((hg, 1, tq), F32)],
        compiler_params=_cparams(("parallel", "parallel", "arbitrary"),
                                 ),
        name="fox_attn",
    )(qa, ka, vt)


def _trunk(x, pos0, chunk, s0, past, p):
    batch, seq_len, d = x.shape
    nh_ret, dk, dv = s0.shape[2:]
    rv = nh_ret * dv
    nh_fox = p['fox_b_f'].shape[0]
    width = p['fox_w_q'].shape[-1]
    pos = pos0 + jnp.arange(seq_len)
    h = x.reshape(batch * seq_len, d)

    h = _ffn(h, p['ffn1_g'][0], p['ffn1_w_in'][0], p['ffn1_w_out'][0])
    q, k, v, sg = _ret_proj(h, p['mix_g'][0], p['ret_w_in'][0], pos, seq_len, nh_ret, rv)
    o, s_fin = _retention(q, k, v, sg, s0[0], p['ret_gn_g'][0], batch, seq_len, chunk)
    h = _proj_res(o, p['ret_w_out'][0], h)
    h = _ffn(h, p['ffn2_g'][0], p['ffn2_w_in'][0], p['ffn2_w_out'][0])

    head_dim = width // nh_fox
    pad_q = -seq_len % LANES
    if past is None:
        assert pad_q == 0
        q_off = 0
        k_new, v_new, logf_pad, ka, vt, cq_t = _fox_kv(h, p['kv_g'], p['fox_w_kvf'], p['fox_b_f'], width, nh_fox,
                                                       batch, seq_len, tk=256)
    else:
        k_new, v_new, logf_pad = _fox_kv(h, p['kv_g'], p['fox_w_kvf'], p['fox_b_f'], width, nh_fox, batch, seq_len)
    k_new = k_new.reshape(batch, seq_len, nh_fox, head_dim)
    v_new = v_new.reshape(batch, seq_len, nh_fox, head_dim)
    logf_new = logf_pad.reshape(batch, seq_len, LANES)[:, :, :nh_fox]
    if past is not None:
        past_len = past[0].shape[1]
        k_all, v_all = (jnp.concatenate([old.reshape(batch, past_len, width), new.reshape(batch, seq_len, width)],
                                        axis=1) for old, new in ((past[0], k_new), (past[1], v_new)))
        logf_all = jnp.concatenate([past[2], logf_new], axis=1)
        n_keys = past_len + seq_len
        q_off = past_len
        pad_k = -n_keys % LANES
        assert pad_q <= pad_k
        if pad_k:
            k_all, v_all, logf_all = (jnp.pad(t, ((0, 0), (0, pad_k), (0, 0))) for t in (k_all, v_all, logf_all))
        n_pad = n_keys + pad_k
        tk = 256 if n_pad % 256 == 0 else n_pad
        cumf = _cumsum_rows(jnp.pad(logf_all, ((0, 0), (0, 0), (0, LANES - nh_fox))))
        ka, vt = _kpack(k_all, v_all, cumf, nh_fox, tk)
        cq_t = jnp.swapaxes(cumf[:, q_off:q_off + seq_len, :nh_fox], 1, 2)

    h = _ffn(h, p['ffn1_g'][1], p['ffn1_w_in'][1], p['ffn1_w_out'][1])
    qa = _fox_q(h, p['mix_g'][1], p['fox_w_q'][0], cq_t, batch, seq_len, nh_fox)
    if pad_q:
        qa = jnp.pad(qa, ((0, 0), (0, 0), (0, 0), (0, pad_q)))
    o = _fox_attn(qa, ka, vt, q_off)
    if pad_q:
        o = o.reshape(batch, seq_len + pad_q, width)[:, :seq_len].reshape(batch * seq_len, width)
    h = _proj_res(o, p['fox_w_out'][0], h)
    y = _ffn(h, p['ffn2_g'][1], p['ffn2_w_in'][1], p['ffn2_w_out'][1], final_g=p['final_g'])

    return (y.reshape(batch, seq_len, d), s_fin[None], k_new, v_new, logf_new)


def kernel(x_prompt, x_sample, state_ret, cache_k, cache_v, cache_logf, ffn1_g, ffn1_w_in, ffn1_w_out, mix_g,
           ffn2_g, ffn2_w_in, ffn2_w_out, ret_w_in, ret_gn_g, ret_w_out, kv_g, fox_w_kvf, fox_b_f, fox_w_q,
           fox_w_out, final_g):
    p = {'ffn1_g': ffn1_g, 'ffn1_w_in': ffn1_w_in, 'ffn1_w_out': ffn1_w_out, 'mix_g': mix_g,
         'ffn2_g': ffn2_g, 'ffn2_w_in': ffn2_w_in, 'ffn2_w_out': ffn2_w_out,
         'ret_w_in': ret_w_in, 'ret_gn_g': ret_gn_g, 'ret_w_out': ret_w_out,
         'kv_g': kv_g, 'fox_w_kvf': fox_w_kvf, 'fox_b_f': fox_b_f,
         'fox_w_q': fox_w_q, 'fox_w_out': fox_w_out, 'final_g': final_g}
    assert state_ret.shape[0] == 1 and fox_w_q.shape[0] == 1
    s0_prompt = jnp.zeros((1, x_prompt.shape[0]) + state_ret.shape[2:], F32)
    y_p, s_p, k_p, v_p, f_p = _trunk(x_prompt, 0, RET_CHUNK, s0_prompt, None, p)
    y_s, s_s, k_s, v_s, f_s = _trunk(x_sample, cache_k.shape[1], x_sample.shape[1], state_ret,
                                     (cache_k, cache_v, cache_logf), p)
    return (y_p, y_s, s_p, k_p, v_p, f_p, s_s, k_s, v_s, f_s)
```

```python
import functools

import jax
import jax.numpy as jnp
from jax import lax
from jax.experimental import pallas as pl
from jax.experimental.pallas import tpu as pltpu

F32 = jnp.float32
BF16 = jnp.bfloat16

EPS = 1e-6
ROPE_BASE = 10000.0
RET_CHUNK = 64
LANES = 128
HALF_LANES = LANES // 2
V7X_VMEM_LIMIT = 56 * 1024 * 1024

AUG_PIECE0 = 64
AUG_QPIECE0 = 112
N_PIECES = 3
VT_ROWS = 80
LOG2E = 1.4426950408889634


def _cparams(sem, flags=None):
    return pltpu.CompilerParams(dimension_semantics=sem, vmem_limit_bytes=V7X_VMEM_LIMIT, flags=flags)


def _resident(shape):
    nd = len(shape)
    return pl.BlockSpec(shape, lambda *_: (0,) * nd, pipeline_mode=pl.Buffered(1))


def _rmsnorm(x, g):
    ms = jnp.mean(x * x, axis=-1, keepdims=True)
    return x * lax.rsqrt(ms + EPS) * g


def _silu(x):
    return x * jax.nn.sigmoid(x)


def _dot(a, b):
    return jnp.dot(a, b, preferred_element_type=F32)


def _dot_nt(a, b):
    return lax.dot_general(a, b, (((1,), (1,)), ((), ())), preferred_element_type=F32)


def _dot_tn(a, b):
    return lax.dot_general(a, b, (((0,), (0,)), ((), ())), preferred_element_type=F32)


def _split3(c):
    hi = c.astype(BF16).astype(F32)
    r = c - hi
    mid = r.astype(BF16).astype(F32)
    lo = (r - mid).astype(BF16).astype(F32)
    return hi, mid, lo


def _row_tile(n, want):
    t = min(n, want)
    assert n % t == 0 and t % 8 == 0, (n, t)
    return t


def _ffn_kernel(x_ref, g_ref, win_ref, wout_ref, fg_ref, o_ref, act_ref, *, d_ff, ck, final_norm):
    x = x_ref[...]
    xn = _rmsnorm(x, g_ref[...]).astype(BF16)
    for c in range(d_ff // ck):
        gate = _dot(xn, win_ref[:, c * ck:(c + 1) * ck])
        up = _dot(xn, win_ref[:, d_ff + c * ck:d_ff + (c + 1) * ck])
        act_ref[:, c * ck:(c + 1) * ck] = (_silu(gate) * up).astype(BF16)
    y = x + 0.5 * _dot(act_ref[...], wout_ref[...])
    if final_norm:
        y = _rmsnorm(y, fg_ref[...])
    o_ref[...] = y


def _ffn(h, g, w_in, w_out, final_g=None):
    n, d = h.shape
    d_ff = w_out.shape[0]
    tm = _row_tile(n, 1024)
    ck = 256
    assert d_ff % ck == 0
    fg = jnp.ones((d,), F32) if final_g is None else final_g
    return pl.pallas_call(
        functools.partial(_ffn_kernel, d_ff=d_ff, ck=ck, final_norm=final_g is not None),
        out_shape=jax.ShapeDtypeStruct((n, d), F32),
        grid=(n // tm,),
        in_specs=[
            pl.BlockSpec((tm, d), lambda i: (i, 0)),
            _resident((1, d)),
            _resident((d, 2 * d_ff)),
            _resident((d_ff, d)),
            _resident((1, d)),
        ],
        out_specs=pl.BlockSpec((tm, d), lambda i: (i, 0)),
        scratch_shapes=[pltpu.VMEM((tm, d_ff), BF16)],
        compiler_params=_cparams(("parallel",)),
        name="ffn",
    )(h, g.reshape(1, d), w_in.astype(BF16), w_out.astype(BF16), fg.reshape(1, d))


def _proj_res_kernel(a_ref, w_ref, r_ref, o_ref):
    o_ref[...] = r_ref[...] + _dot(a_ref[...], w_ref[...])


def _proj_res(a, w, res):
    n, k = a.shape
    d = w.shape[1]
    tm = _row_tile(n, 1024)
    return pl.pallas_call(
        _proj_res_kernel,
        out_shape=jax.ShapeDtypeStruct((n, d), F32),
        grid=(n // tm,),
        in_specs=[
            pl.BlockSpec((tm, k), lambda i: (i, 0)),
            _resident((k, d)),
            pl.BlockSpec((tm, d), lambda i: (i, 0)),
        ],
        out_specs=pl.BlockSpec((tm, d), lambda i: (i, 0)),
        compiler_params=_cparams(("parallel",)),
        name="proj_res",
    )(a, w.astype(BF16), res)


def _rope_tables(pos, half):
    inv = ROPE_BASE ** (-jnp.arange(half, dtype=F32) / half)
    ang = pos.astype(F32)[:, None] * inv[None, :]
    return jnp.cos(ang), jnp.sin(ang)


def _ret_proj_kernel(x_ref, g_ref, w_ref, cos_ref, sin_ref, q_ref, k_ref, v_ref, sg_ref, *, d, nh, rv):
    dk = d // nh
    half = dk // 2
    xn = _rmsnorm(x_ref[...], g_ref[...]).astype(BF16)
    cos = cos_ref[...]
    sin = sin_ref[...]
    for off, out_ref, scale in ((0, q_ref, dk ** -0.5), (d, k_ref, 1.0)):
        t = _dot(xn, w_ref[:, off:off + d])
        for hd in range(nh):
            x1 = t[:, hd * dk:hd * dk + half]
            x2 = t[:, hd * dk + half:(hd + 1) * dk]
            out_ref[:, hd * dk:hd * dk + half] = ((x1 * cos - x2 * sin) * scale).astype(BF16)
            out_ref[:, hd * dk + half:(hd + 1) * dk] = ((x1 * sin + x2 * cos) * scale).astype(BF16)
    v_ref[...] = _dot(xn, w_ref[:, 2 * d:2 * d + rv]).astype(BF16)
    sg_ref[...] = _silu(_dot(xn, w_ref[:, 2 * d + rv:2 * d + 2 * rv])).astype(BF16)


def _ret_proj(h, g, w_in, pos, seq_len, nh, rv):
    n, d = h.shape
    half = d // nh // 2
    tm = _row_tile(n, 512)
    cos, sin = _rope_tables(pos, half)
    if seq_len % tm == 0:
        period = seq_len // tm
    else:
        assert tm % seq_len == 0
        cos, sin = (jnp.tile(t, (tm // seq_len, 1)) for t in (cos, sin))
        period = 1
    tab = pl.BlockSpec((tm, half), lambda i: (i % period, 0))
    row = lambda w: pl.BlockSpec((tm, w), lambda i: (i, 0))
    return pl.pallas_call(
        functools.partial(_ret_proj_kernel, d=d, nh=nh, rv=rv),
        out_shape=(jax.ShapeDtypeStruct((n, d), BF16), jax.ShapeDtypeStruct((n, d), BF16),
                   jax.ShapeDtypeStruct((n, rv), BF16), jax.ShapeDtypeStruct((n, rv), BF16)),
        grid=(n // tm,),
        in_specs=[row(d), _resident((1, d)), _resident((d, 2 * d + 2 * rv)), tab, tab],
        out_specs=(row(d), row(d), row(rv), row(rv)),
        compiler_params=_cparams(("parallel",)),
        name="ret_proj",
    )(h, g.reshape(1, d), w_in.astype(BF16), cos, sin)


def _ret_decay_tables(nh, t, chunk):
    log_gamma = jnp.log1p(-jnp.exp2(-5.0 - jnp.arange(nh, dtype=F32)))
    pos = jnp.arange(t, dtype=F32)
    dist = jnp.abs(pos[:, None] - pos[None, :])
    cid = jnp.arange(t) // chunk
    visible = (cid[None, :] <= cid[:, None]).astype(F32)
    dmask = jnp.exp(log_gamma[:, None, None] * dist) * visible[None]
    qdec = jnp.exp(log_gamma[:, None] * (pos[None, :] + 1.0))[:, :, None]
    kdec = jnp.exp(log_gamma[:, None] * (t - 1.0 - pos[None, :]))[:, :, None]
    sdec = jnp.exp(log_gamma * t)
    return dmask, qdec, kdec, sdec


def _retention_kernel(sdec_ref, q_ref, k_ref, v_ref, sg_ref, dmask_ref, qdec_ref, kdec_ref, gn_ref, s0_ref,
                      o_ref, sout_ref, s_ref, *, nh, dk, dv):
    j = pl.program_id(1)

    @pl.when(j == 0)
    def _():
        s_ref[...] = s0_ref[...]

    for hd in range(nh):
        qh = q_ref[:, hd * dk:(hd + 1) * dk]
        kh = k_ref[:, hd * dk:(hd + 1) * dk]
        vh = v_ref[:, hd * dv:(hd + 1) * dv]
        state = s_ref[hd]
        scores = _dot_nt(qh, kh) * dmask_ref[hd]
        o = _dot(scores.astype(BF16), vh) + _dot(qh, state.astype(BF16)) * qdec_ref[hd]
        kd = (kh.astype(F32) * kdec_ref[hd]).astype(BF16)
        s_ref[hd] = sdec_ref[hd] * state + _dot_tn(kd, vh)
        mu = jnp.mean(o, axis=-1, keepdims=True)
        oc = o - mu
        var = jnp.mean(oc * oc, axis=-1, keepdims=True)
        on = oc * lax.rsqrt(var + EPS) * gn_ref[:, hd * dv:(hd + 1) * dv]
        o_ref[:, hd * dv:(hd + 1) * dv] = (sg_ref[:, hd * dv:(hd + 1) * dv].astype(F32) * on).astype(BF16)

    @pl.when(j == pl.num_programs(1) - 1)
    def _():
        sout_ref[...] = s_ref[...]


def _retention(q, k, v, sg, s0, gn_g, batch, seq_len, chunk):
    nh, dk, dv = s0.shape[1:]
    rv = nh * dv
    t = _row_tile(seq_len, max(chunk, 256))
    assert t % chunk == 0
    nblk = seq_len // t
    dmask, qdec, kdec, sdec = _ret_decay_tables(nh, t, chunk)
    row = lambda w: pl.BlockSpec((t, w), lambda b, j: (b * nblk + j, 0))
    st = pl.BlockSpec((None, nh, dk, dv), lambda b, j: (b, 0, 0, 0))
    return pl.pallas_call(
        functools.partial(_retention_kernel, nh=nh, dk=dk, dv=dv),
        out_shape=(jax.ShapeDtypeStruct((batch * seq_len, rv), BF16),
                   jax.ShapeDtypeStruct((batch, nh, dk, dv), F32)),
        grid=(batch, nblk),
        in_specs=[
            pl.BlockSpec(memory_space=pltpu.SMEM),
            row(nh * dk), row(nh * dk), row(rv), row(rv),
            _resident((nh, t, t)), _resident((nh, t, 1)), _resident((nh, t, 1)), _resident((1, rv)),
            st,
        ],
        out_specs=(row(rv), st),
        scratch_shapes=[pltpu.VMEM((nh, dk, dv), F32)],
        compiler_params=_cparams(("parallel", "arbitrary")),
        name="retention",
    )(sdec, q, k, v, sg, dmask, qdec, kdec, gn_g.reshape(1, rv), s0)


def _lane_iota(rows):
    return lax.broadcasted_iota(jnp.int32, (rows, LANES), 1)


def _head_lanes(x, hd):
    t = x[:, (hd // 2) * LANES:(hd // 2 + 1) * LANES]
    return pltpu.roll(t, HALF_LANES, axis=1) if hd % 2 else t


def _tri_cumsum(tri, x):
    hi, mid, lo = _split3(x)
    return _dot(tri, hi.astype(BF16)) + _dot(tri, mid.astype(BF16)) + _dot(tri, lo.astype(BF16))


def _pack_keys(k, c, ka_ref, nh):
    rows = k.shape[0]
    lane = _lane_iota(rows)
    hi, mid, lo = _split3(c * LOG2E)
    extra = jnp.where(lane < AUG_QPIECE0 + N_PIECES, 1.0, 0.0)
    for p, piece in reversed(list(enumerate((hi, mid, lo)))):
        lo_lane = AUG_PIECE0 + p * nh
        extra = jnp.where(lane < lo_lane + nh, -pltpu.roll(piece, lo_lane, axis=1), extra)
    for hd in range(nh):
        ka_ref[hd] = jnp.where(lane < HALF_LANES, _head_lanes(k, hd), extra).astype(BF16)


def _pack_values(v, vt_ref, nh, tk):
    rows = v.shape[0]
    vt = v.T
    sub = lax.broadcasted_iota(jnp.int32, (VT_ROWS - HALF_LANES, rows), 0)
    ones_row = jnp.where(sub == 0, 1.0, 0.0)
    for hd in range(nh):
        t = jnp.concatenate([vt[hd * HALF_LANES:(hd + 1) * HALF_LANES, :], ones_row], axis=0).astype(BF16)
        for s in range(rows // tk):
            vt_ref[hd, s] = t[:, s * tk:(s + 1) * tk]


def _fox_kv_kernel(x_ref, g_ref, wk_ref, wv_ref, wf_ref, bf_ref, tri_ref, k_ref, v_ref, logf_ref, *rest,
                   nh, tk, packed):
    tm = x_ref.shape[0]
    xn = _rmsnorm(x_ref[...], g_ref[...]).astype(BF16)
    k = _dot(xn, wk_ref[...])
    v = _dot(xn, wv_ref[...])
    z = _dot(xn, wf_ref[...]) + bf_ref[...]
    logf = jnp.minimum(z, 0.0) - jnp.log1p(jnp.exp(-jnp.abs(z)))
    logf_ref[...] = logf
    for src, dst in ((k, k_ref), (v, v_ref)):
        for hd in range(nh):
            dst[pl.ds(hd, tm, stride=nh), :] = _head_lanes(src, hd)[:, :HALF_LANES]
    if packed:
        ka_ref, vt_ref, ct_ref, carry_ref = rest

        @pl.when(pl.program_id(1) == 0)
        def _():
            carry_ref[...] = jnp.zeros_like(carry_ref)

        c = _tri_cumsum(tri_ref[...], jnp.where(_lane_iota(tm) < nh, logf, 0.0)) + carry_ref[...]
        carry_ref[...] = c[tm - 1:tm, :]
        ct_ref[...] = c.T[:nh, :]
        _pack_keys(k, c, ka_ref, nh)
        _pack_values(v, vt_ref, nh, tk)


def _fox_kv(h, g, w_kvf, b_f, width, nh, batch, seq_len, tk=None):
    n, d = h.shape
    packed = tk is not None
    tm = _row_tile(seq_len if packed else n, 512)
    grid = (batch, seq_len // tm) if packed else (n // tm, 1)
    nblk = grid[1]
    assert width == nh * HALF_LANES and N_PIECES * nh <= AUG_QPIECE0 - AUG_PIECE0
    assert nh & (nh - 1) == 0 and AUG_PIECE0 % nh == 0
    wf = jnp.zeros((d, LANES), BF16).at[:, :nh].set(w_kvf[:, 2 * width:].astype(BF16))
    bf = jnp.zeros((1, LANES), F32).at[0, :nh].set(b_f)
    tri = (jnp.arange(tm)[:, None] >= jnp.arange(tm)[None, :]).astype(BF16)
    row = lambda r, w: pl.BlockSpec((r, w), lambda i, j: (i * nblk + j, 0))
    out_shape = [jax.ShapeDtypeStruct((n * nh, HALF_LANES), F32), jax.ShapeDtypeStruct((n * nh, HALF_LANES), F32),
                 jax.ShapeDtypeStruct((n, LANES), F32)]
    out_specs = [row(tm * nh, HALF_LANES), row(tm * nh, HALF_LANES), row(tm, LANES)]
    scratch = []
    if packed:
        assert tm % tk == 0
        out_shape += [jax.ShapeDtypeStruct((batch, nh, seq_len, LANES), BF16),
                      jax.ShapeDtypeStruct((batch, nh, seq_len // tk, VT_ROWS, tk), BF16),
                      jax.ShapeDtypeStruct((batch, nh, seq_len), F32)]
        out_specs += [pl.BlockSpec((None, nh, tm, LANES), lambda i, j: (i, 0, j, 0)),
                      pl.BlockSpec((None, nh, tm // tk, VT_ROWS, tk), lambda i, j: (i, 0, j, 0, 0)),
                      pl.BlockSpec((None, nh, tm), lambda i, j: (i, 0, j))]
        scratch = [pltpu.VMEM((1, LANES), F32)]
    return pl.pallas_call(
        functools.partial(_fox_kv_kernel, nh=nh, tk=tk, packed=packed),
        out_shape=tuple(out_shape),
        grid=grid,
        in_specs=[row(tm, d), _resident((1, d)), _resident((d, width)), _resident((d, width)),
                  _resident((d, LANES)), _resident((1, LANES)), _resident((tm, tm))],
        out_specs=tuple(out_specs),
        scratch_shapes=scratch,
        compiler_params=_cparams(("parallel", "arbitrary")),
        name="fox_kv",
    )(h, g.reshape(1, d), w_kvf[:, :width].astype(BF16), w_kvf[:, width:2 * width].astype(BF16), wf, bf, tri)


def _cumsum_kernel(x_ref, tri_ref, o_ref, *, tc, nblk):
    tri = tri_ref[...]

    def body(i, carry):
        r0 = pl.multiple_of(i * tc, tc)
        c = _tri_cumsum(tri, x_ref[pl.ds(r0, tc), :]) + carry
        o_ref[pl.ds(r0, tc), :] = c
        return c[tc - 1:tc, :]

    lax.fori_loop(0, nblk, body, jnp.zeros((1, x_ref.shape[-1]), F32))


def _cumsum_rows(x):
    b, n, w = x.shape
    tc = 128 if n % 128 == 0 else 64
    assert n % tc == 0
    tri = (jnp.arange(tc)[:, None] >= jnp.arange(tc)[None, :]).astype(BF16)
    blk = pl.BlockSpec((None, n, w), lambda i: (i, 0, 0))
    return pl.pallas_call(
        functools.partial(_cumsum_kernel, tc=tc, nblk=n // tc),
        out_shape=jax.ShapeDtypeStruct((b, n, w), F32),
        grid=(b,),
        in_specs=[blk, _resident((tc, tc))],
        out_specs=blk,
        compiler_params=_cparams(("parallel",)),
        name="cumsum",
    )(x, tri)


def _kpack_kernel(k_ref, v_ref, c_ref, ka_ref, vt_ref, *, nh, tk):
    _pack_keys(k_ref[...], c_ref[...], ka_ref, nh)
    _pack_values(v_ref[...], vt_ref, nh, tk)


def _kpack(k, v, c, nh, tk):
    b, n, width = k.shape
    tm = max(tk, 512) if n % max(tk, 512) == 0 else tk
    assert n % tm == 0 and tm % tk == 0
    row = lambda w: pl.BlockSpec((None, tm, w), lambda i, j: (i, j, 0))
    return pl.pallas_call(
        functools.partial(_kpack_kernel, nh=nh, tk=tk),
        out_shape=(jax.ShapeDtypeStruct((b, nh, n, LANES), BF16),
                   jax.ShapeDtypeStruct((b, nh, n // tk, VT_ROWS, tk), BF16)),
        grid=(b, n // tm),
        in_specs=[row(width), row(width), row(LANES)],
        out_specs=(pl.BlockSpec((None, nh, tm, LANES), lambda i, j: (i, 0, j, 0)),
                   pl.BlockSpec((None, nh, tm // tk, VT_ROWS, tk), lambda i, j: (i, 0, j, 0, 0))),
        compiler_params=_cparams(("parallel", "parallel")),
        name="kpack",
    )(k, v, c)


def _fox_q_kernel(x_ref, g_ref, wt_ref, ct_ref, qa_ref, *, nh):
    tm = x_ref.shape[0]
    xn = _rmsnorm(x_ref[...], g_ref[...]).astype(BF16)
    qt = _dot_nt(wt_ref[...], xn) * (HALF_LANES ** -0.5 * LOG2E)
    hi, mid, lo = _split3(ct_ref[...] * LOG2E)
    sub = lax.broadcasted_iota(jnp.int32, (8, tm), 0)
    rowp = lax.broadcasted_iota(jnp.int32, (N_PIECES * nh, tm), 0)
    tail = jnp.zeros((LANES - AUG_QPIECE0 - 8, tm), F32)
    for hd in range(nh):
        onehot = jnp.where((rowp & (nh - 1)) == hd, 1.0, 0.0)
        own = [jnp.broadcast_to(a[hd:hd + 1, :], (8, tm)) for a in (hi, mid, lo)]
        pieces = jnp.where(sub == 0, own[0], jnp.where(sub == 1, own[1], jnp.where(sub == 2, own[2], 0.0)))
        tile = jnp.concatenate([qt[hd * HALF_LANES:(hd + 1) * HALF_LANES, :], onehot, pieces, tail], axis=0)
        qa_ref[hd] = tile.astype(BF16)


def _fox_q(h, g, w_q, cq_t, batch, seq_len, nh):
    n, d = h.shape
    tm = _row_tile(seq_len, 512)
    nblk = seq_len // tm
    return pl.pallas_call(
        functools.partial(_fox_q_kernel, nh=nh),
        out_shape=jax.ShapeDtypeStruct((batch, nh, LANES, seq_len), BF16),
        grid=(batch, nblk),
        in_specs=[
            pl.BlockSpec((tm, d), lambda b, j: (b * nblk + j, 0)),
            _resident((1, d)),
            _resident((nh * HALF_LANES, d)),
            pl.BlockSpec((None, nh, tm), lambda b, j: (b, 0, j)),
        ],
        out_specs=pl.BlockSpec((None, nh, LANES, tm), lambda b, j: (b, 0, 0, j)),
        compiler_params=_cparams(("parallel", "parallel")),
        name="fox_q",
    )(h, g.reshape(1, d), w_q.T.astype(BF16), cq_t)


def _fox_attn_kernel(q_ref, k_ref, v_ref, o_ref, s0_ref, s1_ref, acc_ref, m_ref, *, hg, tq, tk, q_off,
                     single_block):
    i = pl.program_id(2)
    q0 = q_off + i * tq
    n_full, n_mask = (0, 1) if single_block else (q0 // tk, tq // tk)
    acc_ref[...] = jnp.zeros_like(acc_ref)
    m_ref[...] = jnp.full_like(m_ref, -jnp.inf)
    slots = (s0_ref, s1_ref)

    def scores_head(hh, j, slot, c0=0):
        r0 = pl.multiple_of(j * tk, tk)
        slots[slot][hh, :, c0:] = _dot(k_ref[hh, pl.ds(r0, tk), :], q_ref[hh, :, c0:])

    def absorb_head(hh, j, slot, c0=0, c1=tq, diag=None):
        w = c1 - c0
        s = slots[slot][hh, :, c0:c1]
        if diag is not None:
            kpos = lax.broadcasted_iota(jnp.int32, (tk, w), 0)
            qpos = lax.broadcasted_iota(jnp.int32, (tk, w), 1)
            s = jnp.where(qpos + diag >= kpos, s, -jnp.inf)
        m_prev = m_ref[hh, :, c0:c1]
        m_new = jnp.maximum(m_prev, jnp.max(s, axis=0, keepdims=True))
        alpha = jnp.exp2(m_prev - m_new)
        p = jnp.exp2(s - m_new).astype(BF16)
        acc_ref[hh, :, c0:c1] = alpha * acc_ref[hh, :, c0:c1] + _dot(v_ref[hh, j], p)
        m_ref[hh, :, c0:c1] = m_new

    def scores(j, slot, c0=0):
        for hh in range(hg):
            scores_head(hh, j, slot, c0)

    def absorb(j, slot, c0=0, c1=tq, diag=None):
        for hh in range(hg):
            absorb_head(hh, j, slot, c0, c1, diag)

    def stage(j, slot):
        for hh in range(hg):
            scores_head(hh, j + 1, 1 - slot)
            absorb_head(hh, j, slot)

    scores(0, 0)

    def pair(jj, carry):
        j = 2 * jj
        stage(j, 0)
        stage(j + 1, 1)
        return carry

    lax.fori_loop(0, n_full // 2, pair, 0)
    for t in range(n_mask):
        lo = 0 if single_block else t * tk
        hi = min(lo + tk, tq)
        for hh in range(hg):
            if t + 1 < n_mask:
                scores_head(hh, n_full + t + 1, (t + 1) % 2, hi)
            absorb_head(hh, n_full + t, t % 2, lo, hi, diag=q_off if single_block else 0)
            if hi < tq:
                absorb_head(hh, n_full + t, t % 2, hi, tq)
    for g in range(hg // 2):
        halves = []
        for hh in (2 * g, 2 * g + 1):
            a = acc_ref[hh]
            halves.append(a[:HALF_LANES, :] / a[HALF_LANES:HALF_LANES + 1, :])
        o_ref[:, g * LANES:(g + 1) * LANES] = jnp.concatenate(halves, axis=0).T.astype(BF16)


def _fox_attn(qa, ka, vt, q_off):
    b, nh, _, seq_len = qa.shape
    n_keys = ka.shape[2]
    tk = vt.shape[-1]
    hg = 8
    tq = min(seq_len, 512)
    assert seq_len % tq == 0 and tq % LANES == 0 and nh % hg == 0 and q_off + seq_len <= n_keys
    nq = seq_len // tq
    single_block = n_keys == tk
    assert (single_block and nq == 1) or (tq % (2 * tk) == 0 and q_off % (2 * tk) == 0)
    return pl.pallas_call(
        functools.partial(_fox_attn_kernel, hg=hg, tq=tq, tk=tk, q_off=q_off, single_block=single_block),
        out_shape=jax.ShapeDtypeStruct((b * seq_len, nh * HALF_LANES), BF16),
        grid=(b, nh // hg, nq),
        in_specs=[
            pl.BlockSpec((None, hg, LANES, tq), lambda bi, g, i: (bi, g, 0, i)),
            pl.BlockSpec((None, hg, n_keys, LANES), lambda bi, g, i: (bi, g, 0, 0)),
            pl.BlockSpec((None, hg, n_keys // tk, VT_ROWS, tk), lambda bi, g, i: (bi, g, 0, 0, 0)),
        ],
        out_specs=pl.BlockSpec((tq, hg * HALF_LANES), lambda bi, g, i: (bi * nq + i, g)),
        scratch_shapes=[pltpu.VMEM((hg, tk, tq), F32), pltpu.VMEM((hg, tk, tq), F32),
                        pltpu.VMEM((hg, VT_ROWS, tq), F32), pltpu.VMEM((hg, 1, tq), F32)],
        compiler_params=_cparams(("parallel", "parallel", "arbitrary"),
                                 ),
        name="fox_attn",
    )(qa, ka, vt)


def _trunk(x, pos0, chunk, s0, past, p):
    batch, seq_len, d = x.shape
    nh_ret, dk, dv = s0.shape[2:]
    rv = nh_ret * dv
    nh_fox = p['fox_b_f'].shape[0]
    width = p['fox_w_q'].shape[-1]
    pos = pos0 + jnp.arange(seq_len)
    h = x.reshape(batch * seq_len, d)

    h = _ffn(h, p['ffn1_g'][0], p['ffn1_w_in'][0], p['ffn1_w_out'][0])
    q, k, v, sg = _ret_proj(h, p['mix_g'][0], p['ret_w_in'][0], pos, seq_len, nh_ret, rv)
    o, s_fin = _retention(q, k, v, sg, s0[0], p['ret_gn_g'][0], batch, seq_len, chunk)
    h = _proj_res(o, p['ret_w_out'][0], h)
    h = _ffn(h, p['ffn2_g'][0], p['ffn2_w_in'][0], p['ffn2_w_out'][0])

    head_dim = width // nh_fox
    pad_q = -seq_len % LANES
    if past is None:
        assert pad_q == 0
        q_off = 0
        k_new, v_new, logf_pad, ka, vt, cq_t = _fox_kv(h, p['kv_g'], p['fox_w_kvf'], p['fox_b_f'], width, nh_fox,
                                                       batch, seq_len, tk=256)
    else:
        k_new, v_new, logf_pad = _fox_kv(h, p['kv_g'], p['fox_w_kvf'], p['fox_b_f'], width, nh_fox, batch, seq_len)
    k_new = k_new.reshape(batch, seq_len, nh_fox, head_dim)
    v_new = v_new.reshape(batch, seq_len, nh_fox, head_dim)
    logf_new = logf_pad.reshape(batch, seq_len, LANES)[:, :, :nh_fox]
    if past is not None:
        past_len = past[0].shape[1]
        k_all, v_all = (jnp.concatenate([old.reshape(batch, past_len, width), new.reshape(batch, seq_len, width)],
                                        axis=1) for old, new in ((past[0], k_new), (past[1], v_new)))
        logf_all = jnp.concatenate([past[2], logf_new], axis=1)
        n_keys = past_len + seq_len
        q_off = past_len
        pad_k = -n_keys % LANES
        assert pad_q <= pad_k
        if pad_k:
            k_all, v_all, logf_all = (jnp.pad(t, ((0, 0), (0, pad_k), (0, 0))) for t in (k_all, v_all, logf_all))
        n_pad = n_keys + pad_k
        tk = 256 if n_pad % 256 == 0 else n_pad
        cumf = _cumsum_rows(jnp.pad(logf_all, ((0, 0), (0, 0), (0, LANES - nh_fox))))
        ka, vt = _kpack(k_all, v_all, cumf, nh_fox, tk)
        cq_t = jnp.swapaxes(cumf[:, q_off:q_off + seq_len, :nh_fox], 1, 2)

    h = _ffn(h, p['ffn1_g'][1], p['ffn1_w_in'][1], p['ffn1_w_out'][1])
    qa = _fox_q(h, p['mix_g'][1], p['fox_w_q'][0], cq_t, batch, seq_len, nh_fox)
    if pad_q:
        qa = jnp.pad(qa, ((0, 0), (0, 0), (0, 0), (0, pad_q)))
    o = _fox_attn(qa, ka, vt, q_off)
    if pad_q:
        o = o.reshape(batch, seq_len + pad_q, width)[:, :seq_len].reshape(batch * seq_len, width)
    h = _proj_res(o, p['fox_w_out'][0], h)
    y = _ffn(h, p['ffn2_g'][1], p['ffn2_w_in'][1], p['ffn2_w_out'][1], final_g=p['final_g'])

    return (y.reshape(batch, seq_len, d), s_fin[None], k_new, v_new, logf_new)


def kernel(x_prompt, x_sample, state_ret, cache_k, cache_v, cache_logf, ffn1_g, ffn1_w_in, ffn1_w_out, mix_g,
           ffn2_g, ffn2_w_in, ffn2_w_out, ret_w_in, ret_gn_g, ret_w_out, kv_g, fox_w_kvf, fox_b_f, fox_w_q,
           fox_w_out, final_g):
    p = {'ffn1_g': ffn1_g, 'ffn1_w_in': ffn1_w_in, 'ffn1_w_out': ffn1_w_out, 'mix_g': mix_g,
         'ffn2_g': ffn2_g, 'ffn2_w_in': ffn2_w_in, 'ffn2_w_out': ffn2_w_out,
         'ret_w_in': ret_w_in, 'ret_gn_g': ret_gn_g, 'ret_w_out': ret_w_out,
         'kv_g': kv_g, 'fox_w_kvf': fox_w_kvf, 'fox_b_f': fox_b_f,
         'fox_w_q': fox_w_q, 'fox_w_out': fox_w_out, 'final_g': final_g}
    assert state_ret.shape[0] == 1 and fox_w_q.shape[0] == 1
    s0_prompt = jnp.zeros((1, x_prompt.shape[0]) + state_ret.shape[2:], F32)
    y_p, s_p, k_p, v_p, f_p = _trunk(x_prompt, 0, RET_CHUNK, s0_prompt, None, p)
    y_s, s_s, k_s, v_s, f_s = _trunk(x_sample, cache_k.shape[1], x_sample.shape[1], state_ret,
                                     (cache_k, cache_v, cache_logf), p)
    return (y_p, y_s, s_p, k_p, v_p, f_p, s_s, k_s, v_s, f_s)
```

```python
import functools

import jax
import jax.numpy as jnp
from jax import lax
from jax.experimental import pallas as pl
from jax.experimental.pallas import tpu as pltpu

F32 = jnp.float32
BF16 = jnp.bfloat16

EPS = 1e-6
ROPE_BASE = 10000.0
RET_CHUNK = 64
LANES = 128
HALF_LANES = LANES // 2
V7X_VMEM_LIMIT = 56 * 1024 * 1024

AUG_PIECE0 = 64
AUG_QPIECE0 = 112
N_PIECES = 3
VT_ROWS = 80
LOG2E = 1.4426950408889634


def _cparams(sem, flags=None):
    return pltpu.CompilerParams(dimension_semantics=sem, vmem_limit_bytes=V7X_VMEM_LIMIT, flags=flags)


def _resident(shape):
    nd = len(shape)
    return pl.BlockSpec(shape, lambda *_: (0,) * nd, pipeline_mode=pl.Buffered(1))


def _rmsnorm(x, g):
    ms = jnp.mean(x * x, axis=-1, keepdims=True)
    return x * lax.rsqrt(ms + EPS) * g


def _silu(x):
    return x * jax.nn.sigmoid(x)


def _dot(a, b):
    return jnp.dot(a, b, preferred_element_type=F32)


def _dot_nt(a, b):
    return lax.dot_general(a, b, (((1,), (1,)), ((), ())), preferred_element_type=F32)


def _dot_tn(a, b):
    return lax.dot_general(a, b, (((0,), (0,)), ((), ())), preferred_element_type=F32)


def _split3(c):
    hi = c.astype(BF16).astype(F32)
    r = c - hi
    mid = r.astype(BF16).astype(F32)
    lo = (r - mid).astype(BF16).astype(F32)
    return hi, mid, lo


def _row_tile(n, want):
    t = min(n, want)
    assert n % t == 0 and t % 8 == 0, (n, t)
    return t


def _ffn_kernel(x_ref, g_ref, win_ref, wout_ref, fg_ref, o_ref, act_ref, *, d_ff, ck, final_norm):
    x = x_ref[...]
    xn = _rmsnorm(x, g_ref[...]).astype(BF16)
    for c in range(d_ff // ck):
        gate = _dot(xn, win_ref[:, c * ck:(c + 1) * ck])
        up = _dot(xn, win_ref[:, d_ff + c * ck:d_ff + (c + 1) * ck])
        act_ref[:, c * ck:(c + 1) * ck] = (_silu(gate) * up).astype(BF16)
    y = x + 0.5 * _dot(act_ref[...], wout_ref[...])
    if final_norm:
        y = _rmsnorm(y, fg_ref[...])
    o_ref[...] = y


def _ffn(h, g, w_in, w_out, final_g=None):
    n, d = h.shape
    d_ff = w_out.shape[0]
    tm = _row_tile(n, 1024)
    ck = 256
    assert d_ff % ck == 0
    fg = jnp.ones((d,), F32) if final_g is None else final_g
    return pl.pallas_call(
        functools.partial(_ffn_kernel, d_ff=d_ff, ck=ck, final_norm=final_g is not None),
        out_shape=jax.ShapeDtypeStruct((n, d), F32),
        grid=(n // tm,),
        in_specs=[
            pl.BlockSpec((tm, d), lambda i: (i, 0)),
            _resident((1, d)),
            _resident((d, 2 * d_ff)),
            _resident((d_ff, d)),
            _resident((1, d)),
        ],
        out_specs=pl.BlockSpec((tm, d), lambda i: (i, 0)),
        scratch_shapes=[pltpu.VMEM((tm, d_ff), BF16)],
        compiler_params=_cparams(("parallel",)),
        name="ffn",
    )(h, g.reshape(1, d), w_in.astype(BF16), w_out.astype(BF16), fg.reshape(1, d))


def _proj_res_kernel(a_ref, w_ref, r_ref, o_ref):
    o_ref[...] = r_ref[...] + _dot(a_ref[...], w_ref[...])


def _proj_res(a, w, res):
    n, k = a.shape
    d = w.shape[1]
    tm = _row_tile(n, 1024)
    return pl.pallas_call(
        _proj_res_kernel,
        out_shape=jax.ShapeDtypeStruct((n, d), F32),
        grid=(n // tm,),
        in_specs=[
            pl.BlockSpec((tm, k), lambda i: (i, 0)),
            _resident((k, d)),
            pl.BlockSpec((tm, d), lambda i: (i, 0)),
        ],
        out_specs=pl.BlockSpec((tm, d), lambda i: (i, 0)),
        compiler_params=_cparams(("parallel",)),
        name="proj_res",
    )(a, w.astype(BF16), res)


def _rope_tables(pos, half):
    inv = ROPE_BASE ** (-jnp.arange(half, dtype=F32) / half)
    ang = pos.astype(F32)[:, None] * inv[None, :]
    return jnp.cos(ang), jnp.sin(ang)


def _ret_proj_kernel(x_ref, g_ref, w_ref, cos_ref, sin_ref, q_ref, k_ref, v_ref, sg_ref, *, d, nh, rv):
    dk = d // nh
    half = dk // 2
    xn = _rmsnorm(x_ref[...], g_ref[...]).astype(BF16)
    cos = cos_ref[...]
    sin = sin_ref[...]
    sg_ref[...] = _silu(_dot(xn, w_ref[:, 2 * d + rv:2 * d + 2 * rv])).astype(BF16)
    for off, out_ref, scale in ((0, q_ref, dk ** -0.5), (d, k_ref, 1.0)):
        t = _dot(xn, w_ref[:, off:off + d])
        for hd in range(nh):
            x1 = t[:, hd * dk:hd * dk + half]
            x2 = t[:, hd * dk + half:(hd + 1) * dk]
            out_ref[:, hd * dk:hd * dk + half] = ((x1 * cos - x2 * sin) * scale).astype(BF16)
            out_ref[:, hd * dk + half:(hd + 1) * dk] = ((x1 * sin + x2 * cos) * scale).astype(BF16)
    v_ref[...] = _dot(xn, w_ref[:, 2 * d:2 * d + rv]).astype(BF16)


def _ret_proj(h, g, w_in, pos, seq_len, nh, rv):
    n, d = h.shape
    half = d // nh // 2
    tm = _row_tile(n, 512)
    cos, sin = _rope_tables(pos, half)
    if seq_len % tm == 0:
        period = seq_len // tm
    else:
        assert tm % seq_len == 0
        cos, sin = (jnp.tile(t, (tm // seq_len, 1)) for t in (cos, sin))
        period = 1
    tab = pl.BlockSpec((tm, half), lambda i: (i % period, 0))
    row = lambda w: pl.BlockSpec((tm, w), lambda i: (i, 0))
    return pl.pallas_call(
        functools.partial(_ret_proj_kernel, d=d, nh=nh, rv=rv),
        out_shape=(jax.ShapeDtypeStruct((n, d), BF16), jax.ShapeDtypeStruct((n, d), BF16),
                   jax.ShapeDtypeStruct((n, rv), BF16), jax.ShapeDtypeStruct((n, rv), BF16)),
        grid=(n // tm,),
        in_specs=[row(d), _resident((1, d)), _resident((d, 2 * d + 2 * rv)), tab, tab],
        out_specs=(row(d), row(d), row(rv), row(rv)),
        compiler_params=_cparams(("parallel",)),
        name="ret_proj",
    )(h, g.reshape(1, d), w_in.astype(BF16), cos, sin)


def _ret_decay_tables(nh, t, chunk):
    log_gamma = jnp.log1p(-jnp.exp2(-5.0 - jnp.arange(nh, dtype=F32)))
    pos = jnp.arange(t, dtype=F32)
    dist = jnp.abs(pos[:, None] - pos[None, :])
    cid = jnp.arange(t) // chunk
    visible = (cid[None, :] <= cid[:, None]).astype(F32)
    dmask = jnp.exp(log_gamma[:, None, None] * dist) * visible[None]
    qdec = jnp.exp(log_gamma[:, None] * (pos[None, :] + 1.0))[:, :, None]
    kdec = jnp.exp(log_gamma[:, None] * (t - 1.0 - pos[None, :]))[:, :, None]
    sdec = jnp.exp(log_gamma * t)
    return dmask, qdec, kdec, sdec


def _retention_kernel(sdec_ref, q_ref, k_ref, v_ref, sg_ref, dmask_ref, qdec_ref, kdec_ref, gn_ref, s0_ref,
                      o_ref, sout_ref, s_ref, *, nh, dk, dv):
    j = pl.program_id(1)

    @pl.when(j == 0)
    def _():
        s_ref[...] = s0_ref[...]

    for hd in range(nh):
        qh = q_ref[:, hd * dk:(hd + 1) * dk]
        kh = k_ref[:, hd * dk:(hd + 1) * dk]
        vh = v_ref[:, hd * dv:(hd + 1) * dv]
        state = s_ref[hd]
        scores = _dot_nt(qh, kh) * dmask_ref[hd]
        o = _dot(scores.astype(BF16), vh) + _dot(qh, state.astype(BF16)) * qdec_ref[hd]
        kd = (kh.astype(F32) * kdec_ref[hd]).astype(BF16)
        s_ref[hd] = sdec_ref[hd] * state + _dot_tn(kd, vh)
        mu = jnp.mean(o, axis=-1, keepdims=True)
        oc = o - mu
        var = jnp.mean(oc * oc, axis=-1, keepdims=True)
        on = oc * lax.rsqrt(var + EPS) * gn_ref[:, hd * dv:(hd + 1) * dv]
        o_ref[:, hd * dv:(hd + 1) * dv] = (sg_ref[:, hd * dv:(hd + 1) * dv].astype(F32) * on).astype(BF16)

    @pl.when(j == pl.num_programs(1) - 1)
    def _():
        sout_ref[...] = s_ref[...]


def _retention(q, k, v, sg, s0, gn_g, batch, seq_len, chunk):
    nh, dk, dv = s0.shape[1:]
    rv = nh * dv
    t = _row_tile(seq_len, max(chunk, 256))
    assert t % chunk == 0
    nblk = seq_len // t
    dmask, qdec, kdec, sdec = _ret_decay_tables(nh, t, chunk)
    row = lambda w: pl.BlockSpec((t, w), lambda b, j: (b * nblk + j, 0))
    st = pl.BlockSpec((None, nh, dk, dv), lambda b, j: (b, 0, 0, 0))
    return pl.pallas_call(
        functools.partial(_retention_kernel, nh=nh, dk=dk, dv=dv),
        out_shape=(jax.ShapeDtypeStruct((batch * seq_len, rv), BF16),
                   jax.ShapeDtypeStruct((batch, nh, dk, dv), F32)),
        grid=(batch, nblk),
        in_specs=[
            pl.BlockSpec(memory_space=pltpu.SMEM),
            row(nh * dk), row(nh * dk), row(rv), row(rv),
            _resident((nh, t, t)), _resident((nh, t, 1)), _resident((nh, t, 1)), _resident((1, rv)),
            st,
        ],
        out_specs=(row(rv), st),
        scratch_shapes=[pltpu.VMEM((nh, dk, dv), F32)],
        compiler_params=_cparams(("parallel", "arbitrary")),
        name="retention",
    )(sdec, q, k, v, sg, dmask, qdec, kdec, gn_g.reshape(1, rv), s0)


def _lane_iota(rows):
    return lax.broadcasted_iota(jnp.int32, (rows, LANES), 1)


def _head_lanes(x, hd):
    t = x[:, (hd // 2) * LANES:(hd // 2 + 1) * LANES]
    return pltpu.roll(t, HALF_LANES, axis=1) if hd % 2 else t


def _tri_cumsum(tri, x):
    hi, mid, lo = _split3(x)
    return _dot(tri, hi.astype(BF16)) + _dot(tri, mid.astype(BF16)) + _dot(tri, lo.astype(BF16))


def _pack_keys(k, c, ka_ref, nh):
    rows = k.shape[0]
    lane = _lane_iota(rows)
    hi, mid, lo = _split3(c * LOG2E)
    extra = jnp.where(lane < AUG_QPIECE0 + N_PIECES, 1.0, 0.0)
    for p, piece in reversed(list(enumerate((hi, mid, lo)))):
        lo_lane = AUG_PIECE0 + p * nh
        extra = jnp.where(lane < lo_lane + nh, -pltpu.roll(piece, lo_lane, axis=1), extra)
    for hd in range(nh):
        ka_ref[hd] = jnp.where(lane < HALF_LANES, _head_lanes(k, hd), extra).astype(BF16)


def _pack_values(v, vt_ref, nh, tk):
    rows = v.shape[0]
    vt = v.T
    sub = lax.broadcasted_iota(jnp.int32, (VT_ROWS - HALF_LANES, rows), 0)
    ones_row = jnp.where(sub == 0, 1.0, 0.0)
    for hd in range(nh):
        t = jnp.concatenate([vt[hd * HALF_LANES:(hd + 1) * HALF_LANES, :], ones_row], axis=0).astype(BF16)
        for s in range(rows // tk):
            vt_ref[hd, s] = t[:, s * tk:(s + 1) * tk]


def _fox_kv_kernel(x_ref, g_ref, wk_ref, wv_ref, wf_ref, bf_ref, tri_ref, k_ref, v_ref, logf_ref, *rest,
                   nh, tk, packed):
    tm = x_ref.shape[0]
    ts = tri_ref.shape[0]
    if packed:
        ka_ref, vt_ref, ct_ref, carry_ref = rest

        @pl.when(pl.program_id(1) == 0)
        def _():
            carry_ref[...] = jnp.zeros_like(carry_ref)

    def project(r):
        rows = pl.ds(r * ts, ts)
        xn = _rmsnorm(x_ref[rows, :], g_ref[...]).astype(BF16)
        k = _dot(xn, wk_ref[...])
        v = _dot(xn, wv_ref[...])
        z = _dot(xn, wf_ref[...]) + bf_ref[...]
        return k, v, jnp.minimum(z, 0.0) - jnp.log1p(jnp.exp(-jnp.abs(z)))

    def emit(r, k, v, logf):
        logf_ref[pl.ds(r * ts, ts), :] = logf
        for src, dst in ((k, k_ref), (v, v_ref)):
            for hd in range(nh):
                dst[pl.ds(r * ts * nh + hd, ts, stride=nh), :] = _head_lanes(src, hd)[:, :HALF_LANES]
        if packed:
            c = _tri_cumsum(tri_ref[...], jnp.where(_lane_iota(ts) < nh, logf, 0.0)) + carry_ref[...]
            carry_ref[...] = c[ts - 1:ts, :]
            ct_ref[:, pl.ds(r * ts, ts)] = c.T[:nh, :]
            _pack_keys(k, c, ka_ref.at[:, pl.ds(r * ts, ts), :], nh)
            _pack_values(v, vt_ref.at[:, pl.ds(r * (ts // tk), ts // tk)], nh, tk)

    pending = project(0)
    for r in range(tm // ts):
        done = pending
        if r + 1 < tm // ts:
            pending = project(r + 1)
        emit(r, *done)


def _fox_kv(h, g, w_kvf, b_f, width, nh, batch, seq_len, tk=None):
    n, d = h.shape
    packed = tk is not None
    tm = _row_tile(seq_len if packed else n, 512)
    grid = (batch, seq_len // tm) if packed else (n // tm, 1)
    nblk = grid[1]
    assert width == nh * HALF_LANES and N_PIECES * nh <= AUG_QPIECE0 - AUG_PIECE0
    assert nh & (nh - 1) == 0 and AUG_PIECE0 % nh == 0
    wf = jnp.zeros((d, LANES), BF16).at[:, :nh].set(w_kvf[:, 2 * width:].astype(BF16))
    bf = jnp.zeros((1, LANES), F32).at[0, :nh].set(b_f)
    ts = min(tm, 256)
    assert tm % ts == 0
    tri = (jnp.arange(ts)[:, None] >= jnp.arange(ts)[None, :]).astype(BF16)
    row = lambda r, w: pl.BlockSpec((r, w), lambda i, j: (i * nblk + j, 0))
    out_shape = [jax.ShapeDtypeStruct((n * nh, HALF_LANES), F32), jax.ShapeDtypeStruct((n * nh, HALF_LANES), F32),
                 jax.ShapeDtypeStruct((n, LANES), F32)]
    out_specs = [row(tm * nh, HALF_LANES), row(tm * nh, HALF_LANES), row(tm, LANES)]
    scratch = []
    if packed:
        assert ts % tk == 0
        out_shape += [jax.ShapeDtypeStruct((batch, nh, seq_len, LANES), BF16),
                      jax.ShapeDtypeStruct((batch, nh, seq_len // tk, VT_ROWS, tk), BF16),
                      jax.ShapeDtypeStruct((batch, nh, seq_len), F32)]
        out_specs += [pl.BlockSpec((None, nh, tm, LANES), lambda i, j: (i, 0, j, 0)),
                      pl.BlockSpec((None, nh, tm // tk, VT_ROWS, tk), lambda i, j: (i, 0, j, 0, 0)),
                      pl.BlockSpec((None, nh, tm), lambda i, j: (i, 0, j))]
        scratch = [pltpu.VMEM((1, LANES), F32)]
    return pl.pallas_call(
        functools.partial(_fox_kv_kernel, nh=nh, tk=tk, packed=packed),
        out_shape=tuple(out_shape),
        grid=grid,
        in_specs=[row(tm, d), _resident((1, d)), _resident((d, width)), _resident((d, width)),
                  _resident((d, LANES)), _resident((1, LANES)), _resident((ts, ts))],
        out_specs=tuple(out_specs),
        scratch_shapes=scratch,
        compiler_params=_cparams(("parallel", "arbitrary")),
        name="fox_kv",
    )(h, g.reshape(1, d), w_kvf[:, :width].astype(BF16), w_kvf[:, width:2 * width].astype(BF16), wf, bf, tri)


def _cumsum_kernel(x_ref, tri_ref, o_ref, *, tc, nblk):
    tri = tri_ref[...]

    def body(i, carry):
        r0 = pl.multiple_of(i * tc, tc)
        c = _tri_cumsum(tri, x_ref[pl.ds(r0, tc), :]) + carry
        o_ref[pl.ds(r0, tc), :] = c
        return c[tc - 1:tc, :]

    lax.fori_loop(0, nblk, body, jnp.zeros((1, x_ref.shape[-1]), F32))


def _cumsum_rows(x):
    b, n, w = x.shape
    tc = 128 if n % 128 == 0 else 64
    assert n % tc == 0
    tri = (jnp.arange(tc)[:, None] >= jnp.arange(tc)[None, :]).astype(BF16)
    blk = pl.BlockSpec((None, n, w), lambda i: (i, 0, 0))
    return pl.pallas_call(
        functools.partial(_cumsum_kernel, tc=tc, nblk=n // tc),
        out_shape=jax.ShapeDtypeStruct((b, n, w), F32),
        grid=(b,),
        in_specs=[blk, _resident((tc, tc))],
        out_specs=blk,
        compiler_params=_cparams(("parallel",)),
        name="cumsum",
    )(x, tri)


def _kpack_kernel(k_ref, v_ref, c_ref, ka_ref, vt_ref, *, nh, tk):
    _pack_keys(k_ref[...], c_ref[...], ka_ref, nh)
    _pack_values(v_ref[...], vt_ref, nh, tk)


def _kpack(k, v, c, nh, tk):
    b, n, width = k.shape
    tm = max(tk, 512) if n % max(tk, 512) == 0 else tk
    assert n % tm == 0 and tm % tk == 0
    row = lambda w: pl.BlockSpec((None, tm, w), lambda i, j: (i, j, 0))
    return pl.pallas_call(
        functools.partial(_kpack_kernel, nh=nh, tk=tk),
        out_shape=(jax.ShapeDtypeStruct((b, nh, n, LANES), BF16),
                   jax.ShapeDtypeStruct((b, nh, n // tk, VT_ROWS, tk), BF16)),
        grid=(b, n // tm),
        in_specs=[row(width), row(width), row(LANES)],
        out_specs=(pl.BlockSpec((None, nh, tm, LANES), lambda i, j: (i, 0, j, 0)),
                   pl.BlockSpec((None, nh, tm // tk, VT_ROWS, tk), lambda i, j: (i, 0, j, 0, 0))),
        compiler_params=_cparams(("parallel", "parallel")),
        name="kpack",
    )(k, v, c)


def _fox_q_kernel(x_ref, g_ref, wt_ref, ct_ref, qa_ref, *, nh):
    tm = x_ref.shape[0]
    xn = _rmsnorm(x_ref[...], g_ref[...]).astype(BF16)
    qt = _dot_nt(wt_ref[...], xn) * (HALF_LANES ** -0.5 * LOG2E)
    hi, mid, lo = _split3(ct_ref[...] * LOG2E)
    sub = lax.broadcasted_iota(jnp.int32, (8, tm), 0)
    rowp = lax.broadcasted_iota(jnp.int32, (N_PIECES * nh, tm), 0)
    tail = jnp.zeros((LANES - AUG_QPIECE0 - 8, tm), F32)
    for hd in range(nh):
        onehot = jnp.where((rowp & (nh - 1)) == hd, 1.0, 0.0)
        own = [jnp.broadcast_to(a[hd:hd + 1, :], (8, tm)) for a in (hi, mid, lo)]
        pieces = jnp.where(sub == 0, own[0], jnp.where(sub == 1, own[1], jnp.where(sub == 2, own[2], 0.0)))
        tile = jnp.concatenate([qt[hd * HALF_LANES:(hd + 1) * HALF_LANES, :], onehot, pieces, tail], axis=0)
        qa_ref[hd] = tile.astype(BF16)


def _fox_q(h, g, w_q, cq_t, batch, seq_len, nh):
    n, d = h.shape
    tm = _row_tile(seq_len, 512)
    nblk = seq_len // tm
    return pl.pallas_call(
        functools.partial(_fox_q_kernel, nh=nh),
        out_shape=jax.ShapeDtypeStruct((batch, nh, LANES, seq_len), BF16),
        grid=(batch, nblk),
        in_specs=[
            pl.BlockSpec((tm, d), lambda b, j: (b * nblk + j, 0)),
            _resident((1, d)),
            _resident((nh * HALF_LANES, d)),
            pl.BlockSpec((None, nh, tm), lambda b, j: (b, 0, j)),
        ],
        out_specs=pl.BlockSpec((None, nh, LANES, tm), lambda b, j: (b, 0, 0, j)),
        compiler_params=_cparams(("parallel", "parallel")),
        name="fox_q",
    )(h, g.reshape(1, d), w_q.T.astype(BF16), cq_t)


def _fox_attn_kernel(q_ref, k_ref, v_ref, o_ref, s0_ref, s1_ref, acc_ref, m_ref, *, hg, tq, tk, q_off,
                     single_block):
    i = pl.program_id(2)
    q0 = q_off + i * tq
    n_full, n_mask = (0, 1) if single_block else (q0 // tk, tq // tk)
    acc_ref[...] = jnp.zeros_like(acc_ref)
    m_ref[...] = jnp.full_like(m_ref, -jnp.inf)
    slots = (s0_ref, s1_ref)
    cw = min(tq, 256)

    def scores_head(hh, j, slot, c0=0, c1=tq):
        r0 = pl.multiple_of(j * tk, tk)
        slots[slot][hh, :, c0:c1] = _dot(k_ref[hh, pl.ds(r0, tk), :], q_ref[hh, :, c0:c1])

    def absorb_head(hh, j, slot, c0=0, c1=tq, diag=None):
        w = c1 - c0
        s = slots[slot][hh, :, c0:c1]
        if diag is not None:
            kpos = lax.broadcasted_iota(jnp.int32, (tk, w), 0)
            qpos = lax.broadcasted_iota(jnp.int32, (tk, w), 1)
            s = jnp.where(qpos + diag >= kpos, s, -jnp.inf)
        m_prev = m_ref[hh, :, c0:c1]
        m_new = jnp.maximum(m_prev, jnp.max(s, axis=0, keepdims=True))
        alpha = jnp.exp2(m_prev - m_new)
        p = jnp.exp2(s - m_new).astype(BF16)
        acc_ref[hh, :, c0:c1] = alpha * acc_ref[hh, :, c0:c1] + _dot(v_ref[hh, j], p)
        m_ref[hh, :, c0:c1] = m_new

    def scores(j, slot, c0=0):
        for hh in range(hg):
            scores_head(hh, j, slot, c0)

    def absorb(j, slot, c0=0, c1=tq, diag=None):
        for hh in range(hg):
            absorb_head(hh, j, slot, c0, c1, diag)

    def stage(j, slot):
        for hh in range(hg):
            for c0 in range(0, tq, cw):
                scores_head(hh, j + 1, 1 - slot, c0, c0 + cw)
                absorb_head(hh, j, slot, c0, c0 + cw)

    scores(0, 0)

    def quad(jj, carry):
        for u in range(4):
            stage(4 * jj + u, u % 2)
        return carry

    lax.fori_loop(0, n_full // 4, quad, 0)

    @pl.when(n_full % 4 == 2)
    def _():
        stage(n_full - 2, 0)
        stage(n_full - 1, 1)
    for t in range(n_mask):
        lo = 0 if single_block else t * tk
        hi = min(lo + tk, tq)
        for hh in range(hg):
            if t + 1 < n_mask:
                scores_head(hh, n_full + t + 1, (t + 1) % 2, hi)
            absorb_head(hh, n_full + t, t % 2, lo, hi, diag=q_off if single_block else 0)
            if hi < tq:
                absorb_head(hh, n_full + t, t % 2, hi, tq)
    for g in range(hg // 2):
        halves = []
        for hh in (2 * g, 2 * g + 1):
            a = acc_ref[hh]
            halves.append(a[:HALF_LANES, :] / a[HALF_LANES:HALF_LANES + 1, :])
        o_ref[:, g * LANES:(g + 1) * LANES] = jnp.concatenate(halves, axis=0).T.astype(BF16)


def _fox_attn(qa, ka, vt, q_off):
    b, nh, _, seq_len = qa.shape
    n_keys = ka.shape[2]
    tk = vt.shape[-1]
    hg = 8
    tq = min(seq_len, 512)
    assert seq_len % tq == 0 and tq % LANES == 0 and nh % hg == 0 and q_off + seq_len <= n_keys
    nq = seq_len // tq
    single_block = n_keys == tk
    assert (single_block and nq == 1) or (tq % (2 * tk) == 0 and q_off % (2 * tk) == 0)
    return pl.pallas_call(
        functools.partial(_fox_attn_kernel, hg=hg, tq=tq, tk=tk, q_off=q_off, single_block=single_block),
        out_shape=jax.ShapeDtypeStruct((b * seq_len, nh * HALF_LANES), BF16),
        grid=(b, nh // hg, nq),
        in_specs=[
            pl.BlockSpec((None, hg, LANES, tq), lambda bi, g, i: (bi, g, 0, i)),
            pl.BlockSpec((None, hg, n_keys, LANES), lambda bi, g, i: (bi, g, 0, 0)),
            pl.BlockSpec((None, hg, n_keys // tk, VT_ROWS, tk), lambda bi, g, i: (bi, g, 0, 0, 0)),
        ],
        out_specs=pl.BlockSpec((tq, hg * HALF_LANES), lambda bi, g, i: (bi * nq + i, g)),
        scratch_shapes=[pltpu.VMEM((hg, tk, tq), F32), pltpu.VMEM((hg, tk, tq), F32),
                        pltpu.VMEM((hg, VT_ROWS, tq), F32), pltpu.VMEM((hg, 1, tq), F32)],
        compiler_params=_cparams(("parallel", "parallel", "arbitrary"),
                                 ),
        name="fox_attn",
    )(qa, ka, vt)


def _trunk(x, pos0, chunk, s0, past, p):
    batch, seq_len, d = x.shape
    nh_ret, dk, dv = s0.shape[2:]
    rv = nh_ret * dv
    nh_fox = p['fox_b_f'].shape[0]
    width = p['fox_w_q'].shape[-1]
    pos = pos0 + jnp.arange(seq_len)
    h = x.reshape(batch * seq_len, d)

    h = _ffn(h, p['ffn1_g'][0], p['ffn1_w_in'][0], p['ffn1_w_out'][0])
    q, k, v, sg = _ret_proj(h, p['mix_g'][0], p['ret_w_in'][0], pos, seq_len, nh_ret, rv)
    o, s_fin = _retention(q, k, v, sg, s0[0], p['ret_gn_g'][0], batch, seq_len, chunk)
    h = _proj_res(o, p['ret_w_out'][0], h)
    h = _ffn(h, p['ffn2_g'][0], p['ffn2_w_in'][0], p['ffn2_w_out'][0])

    head_dim = width // nh_fox
    pad_q = -seq_len % LANES
    if past is None:
        assert pad_q == 0
        q_off = 0
        k_new, v_new, logf_pad, ka, vt, cq_t = _fox_kv(h, p['kv_g'], p['fox_w_kvf'], p['fox_b_f'], width, nh_fox,
                                                       batch, seq_len, tk=256)
    else:
        k_new, v_new, logf_pad = _fox_kv(h, p['kv_g'], p['fox_w_kvf'], p['fox_b_f'], width, nh_fox, batch, seq_len)
    k_new = k_new.reshape(batch, seq_len, nh_fox, head_dim)
    v_new = v_new.reshape(batch, seq_len, nh_fox, head_dim)
    logf_new = logf_pad.reshape(batch, seq_len, LANES)[:, :, :nh_fox]
    if past is not None:
        past_len = past[0].shape[1]
        k_all, v_all = (jnp.concatenate([old.reshape(batch, past_len, width), new.reshape(batch, seq_len, width)],
                                        axis=1) for old, new in ((past[0], k_new), (past[1], v_new)))
        logf_all = jnp.concatenate([past[2], logf_new], axis=1)
        n_keys = past_len + seq_len
        q_off = past_len
        pad_k = -n_keys % LANES
        assert pad_q <= pad_k
        if pad_k:
            k_all, v_all, logf_all = (jnp.pad(t, ((0, 0), (0, pad_k), (0, 0))) for t in (k_all, v_all, logf_all))
        n_pad = n_keys + pad_k
        tk = 256 if n_pad % 256 == 0 else n_pad
        cumf = _cumsum_rows(jnp.pad(logf_all, ((0, 0), (0, 0), (0, LANES - nh_fox))))
        ka, vt = _kpack(k_all, v_all, cumf, nh_fox, tk)
        cq_t = jnp.swapaxes(cumf[:, q_off:q_off + seq_len, :nh_fox], 1, 2)

    h = _ffn(h, p['ffn1_g'][1], p['ffn1_w_in'][1], p['ffn1_w_out'][1])
    qa = _fox_q(h, p['mix_g'][1], p['fox_w_q'][0], cq_t, batch, seq_len, nh_fox)
    if pad_q:
        qa = jnp.pad(qa, ((0, 0), (0, 0), (0, 0), (0, pad_q)))
    o = _fox_attn(qa, ka, vt, q_off)
    if pad_q:
        o = o.reshape(batch, seq_len + pad_q, width)[:, :seq_len].reshape(batch * seq_len, width)
    h = _proj_res(o, p['fox_w_out'][0], h)
    y = _ffn(h, p['ffn2_g'][1], p['ffn2_w_in'][1], p['ffn2_w_out'][1], final_g=p['final_g'])

    return (y.reshape(batch, seq_len, d), s_fin[None], k_new, v_new, logf_new)


def kernel(x_prompt, x_sample, state_ret, cache_k, cache_v, cache_logf, ffn1_g, ffn1_w_in, ffn1_w_out, mix_g,
           ffn2_g, ffn2_w_in, ffn2_w_out, ret_w_in, ret_gn_g, ret_w_out, kv_g, fox_w_kvf, fox_b_f, fox_w_q,
           fox_w_out, final_g):
    p = {'ffn1_g': ffn1_g, 'ffn1_w_in': ffn1_w_in, 'ffn1_w_out': ffn1_w_out, 'mix_g': mix_g,
         'ffn2_g': ffn2_g, 'ffn2_w_in': ffn2_w_in, 'ffn2_w_out': ffn2_w_out,
         'ret_w_in': ret_w_in, 'ret_gn_g': ret_gn_g, 'ret_w_out': ret_w_out,
         'kv_g': kv_g, 'fox_w_kvf': fox_w_kvf, 'fox_b_f': fox_b_f,
         'fox_w_q': fox_w_q, 'fox_w_out': fox_w_out, 'final_g': final_g}
    assert state_ret.shape[0] == 1 and fox_w_q.shape[0] == 1
    s0_prompt = jnp.zeros((1, x_prompt.shape[0]) + state_ret.shape[2:], F32)
    y_p, s_p, k_p, v_p, f_p = _trunk(x_prompt, 0, RET_CHUNK, s0_prompt, None, p)
    y_s, s_s, k_s, v_s, f_s = _trunk(x_sample, cache_k.shape[1], x_sample.shape[1], state_ret,
                                     (cache_k, cache_v, cache_logf), p)
    return (y_p, y_s, s_p, k_p, v_p, f_p, s_s, k_s, v_s, f_s)
```

```python
import functools

import jax
import jax.numpy as jnp
from jax import lax
from jax.experimental import pallas as pl
from jax.experimental.pallas import tpu as pltpu

F32 = jnp.float32
BF16 = jnp.bfloat16

EPS = 1e-6
ROPE_BASE = 10000.0
RET_CHUNK = 64
LANES = 128
HALF_LANES = LANES // 2
V7X_VMEM_LIMIT = 56 * 1024 * 1024
CAST_TILE_BYTES = 6 * 1024 * 1024

AUG_PIECE0 = 64
AUG_QPIECE0 = 112
N_PIECES = 3
VT_ROWS = 80
LOG2E = 1.4426950408889634


def _cparams(sem, flags=None):
    return pltpu.CompilerParams(dimension_semantics=sem, vmem_limit_bytes=V7X_VMEM_LIMIT, flags=flags)


def _resident(shape):
    nd = len(shape)
    return pl.BlockSpec(shape, lambda *_: (0,) * nd, pipeline_mode=pl.Buffered(1))


def _rmsnorm(x, g):
    ms = jnp.mean(x * x, axis=-1, keepdims=True)
    return x * lax.rsqrt(ms + EPS) * g


def _silu(x):
    return x * jax.nn.sigmoid(x)


def _dot(a, b):
    return jnp.dot(a, b, preferred_element_type=F32)


def _dot_nt(a, b):
    return lax.dot_general(a, b, (((1,), (1,)), ((), ())), preferred_element_type=F32)


def _dot_tn(a, b):
    return lax.dot_general(a, b, (((0,), (0,)), ((), ())), preferred_element_type=F32)


def _split3(c):
    hi = c.astype(BF16).astype(F32)
    r = c - hi
    mid = r.astype(BF16).astype(F32)
    lo = (r - mid).astype(BF16).astype(F32)
    return hi, mid, lo


def _row_tile(n, want):
    t = min(n, want)
    assert n % t == 0 and t % 8 == 0, (n, t)
    return t


def _cast_kernel(x_ref, o_ref):
    o_ref[...] = x_ref[...].astype(o_ref.dtype)


def _to_bf16(w):
    w2 = w.reshape(-1, w.shape[-1])
    r, c = w2.shape
    max_rows = max(16, CAST_TILE_BYTES // (4 * c))
    tr = max(t for t in range(16, r + 1, 16) if r % t == 0 and t <= max_rows)
    out = pl.pallas_call(
        _cast_kernel,
        out_shape=jax.ShapeDtypeStruct((r, c), BF16),
        grid=(r // tr,),
        in_specs=[pl.BlockSpec((tr, c), lambda i: (i, 0))],
        out_specs=pl.BlockSpec((tr, c), lambda i: (i, 0)),
        compiler_params=_cparams(("parallel",)),
        name="cast_bf16",
    )(w2)
    return out.reshape(w.shape)


def _ffn_kernel(x_ref, g_ref, win_ref, wout_ref, fg_ref, o_ref, act_ref, *, d_ff, ck, final_norm):
    x = x_ref[...]
    xn = _rmsnorm(x, g_ref[...]).astype(BF16)
    for c in range(d_ff // ck):
        gate = _dot(xn, win_ref[:, c * ck:(c + 1) * ck])
        up = _dot(xn, win_ref[:, d_ff + c * ck:d_ff + (c + 1) * ck])
        act_ref[:, c * ck:(c + 1) * ck] = (_silu(gate) * up).astype(BF16)
    y = x + 0.5 * _dot(act_ref[...], wout_ref[...])
    if final_norm:
        y = _rmsnorm(y, fg_ref[...])
    o_ref[...] = y


def _ffn(h, g, w_in, w_out, final_g=None):
    n, d = h.shape
    d_ff = w_out.shape[0]
    tm = _row_tile(n, 1024)
    ck = 256
    assert d_ff % ck == 0
    fg = jnp.ones((d,), F32) if final_g is None else final_g
    return pl.pallas_call(
        functools.partial(_ffn_kernel, d_ff=d_ff, ck=ck, final_norm=final_g is not None),
        out_shape=jax.ShapeDtypeStruct((n, d), F32),
        grid=(n // tm,),
        in_specs=[
            pl.BlockSpec((tm, d), lambda i: (i, 0)),
            _resident((1, d)),
            _resident((d, 2 * d_ff)),
            _resident((d_ff, d)),
            _resident((1, d)),
        ],
        out_specs=pl.BlockSpec((tm, d), lambda i: (i, 0)),
        scratch_shapes=[pltpu.VMEM((tm, d_ff), BF16)],
        compiler_params=_cparams(("parallel",)),
        name="ffn",
    )(h, g.reshape(1, d), w_in.astype(BF16), w_out.astype(BF16), fg.reshape(1, d))


def _proj_res_kernel(a_ref, w_ref, r_ref, o_ref):
    o_ref[...] = r_ref[...] + _dot(a_ref[...], w_ref[...])


def _proj_res(a, w, res):
    n, k = a.shape
    d = w.shape[1]
    tm = _row_tile(n, 1024)
    return pl.pallas_call(
        _proj_res_kernel,
        out_shape=jax.ShapeDtypeStruct((n, d), F32),
        grid=(n // tm,),
        in_specs=[
            pl.BlockSpec((tm, k), lambda i: (i, 0)),
            _resident((k, d)),
            pl.BlockSpec((tm, d), lambda i: (i, 0)),
        ],
        out_specs=pl.BlockSpec((tm, d), lambda i: (i, 0)),
        compiler_params=_cparams(("parallel",)),
        name="proj_res",
    )(a, w.astype(BF16), res)


def _rope_tables(pos, half):
    inv = ROPE_BASE ** (-jnp.arange(half, dtype=F32) / half)
    ang = pos.astype(F32)[:, None] * inv[None, :]
    return jnp.cos(ang), jnp.sin(ang)


def _ret_proj_kernel(x_ref, g_ref, w_ref, cos_ref, sin_ref, q_ref, k_ref, v_ref, sg_ref, *, d, nh, rv):
    dk = d // nh
    half = dk // 2
    xn = _rmsnorm(x_ref[...], g_ref[...]).astype(BF16)
    cos = cos_ref[...]
    sin = sin_ref[...]
    sg_ref[...] = _silu(_dot(xn, w_ref[:, 2 * d + rv:2 * d + 2 * rv])).astype(BF16)
    for off, out_ref, scale in ((0, q_ref, dk ** -0.5), (d, k_ref, 1.0)):
        t = _dot(xn, w_ref[:, off:off + d])
        for hd in range(nh):
            x1 = t[:, hd * dk:hd * dk + half]
            x2 = t[:, hd * dk + half:(hd + 1) * dk]
            out_ref[:, hd * dk:hd * dk + half] = ((x1 * cos - x2 * sin) * scale).astype(BF16)
            out_ref[:, hd * dk + half:(hd + 1) * dk] = ((x1 * sin + x2 * cos) * scale).astype(BF16)
    v_ref[...] = _dot(xn, w_ref[:, 2 * d:2 * d + rv]).astype(BF16)


def _ret_proj(h, g, w_in, pos, seq_len, nh, rv):
    n, d = h.shape
    half = d // nh // 2
    tm = _row_tile(n, 512)
    cos, sin = _rope_tables(pos, half)
    if seq_len % tm == 0:
        period = seq_len // tm
    else:
        assert tm % seq_len == 0
        cos, sin = (jnp.tile(t, (tm // seq_len, 1)) for t in (cos, sin))
        period = 1
    tab = pl.BlockSpec((tm, half), lambda i: (i % period, 0))
    row = lambda w: pl.BlockSpec((tm, w), lambda i: (i, 0))
    return pl.pallas_call(
        functools.partial(_ret_proj_kernel, d=d, nh=nh, rv=rv),
        out_shape=(jax.ShapeDtypeStruct((n, d), BF16), jax.ShapeDtypeStruct((n, d), BF16),
                   jax.ShapeDtypeStruct((n, rv), BF16), jax.ShapeDtypeStruct((n, rv), BF16)),
        grid=(n // tm,),
        in_specs=[row(d), _resident((1, d)), _resident((d, 2 * d + 2 * rv)), tab, tab],
        out_specs=(row(d), row(d), row(rv), row(rv)),
        compiler_params=_cparams(("parallel",)),
        name="ret_proj",
    )(h, g.reshape(1, d), w_in.astype(BF16), cos, sin)


def _ret_decay_tables(nh, t, chunk):
    log_gamma = jnp.log1p(-jnp.exp2(-5.0 - jnp.arange(nh, dtype=F32)))
    pos = jnp.arange(t, dtype=F32)
    dist = jnp.abs(pos[:, None] - pos[None, :])
    cid = jnp.arange(t) // chunk
    visible = (cid[None, :] <= cid[:, None]).astype(F32)
    dmask = jnp.exp(log_gamma[:, None, None] * dist) * visible[None]
    qdec = jnp.exp(log_gamma[:, None] * (pos[None, :] + 1.0))[:, :, None]
    kdec = jnp.exp(log_gamma[:, None] * (t - 1.0 - pos[None, :]))[:, :, None]
    sdec = jnp.exp(log_gamma * t)
    return dmask, qdec, kdec, sdec


def _retention_kernel(sdec_ref, q_ref, k_ref, v_ref, sg_ref, dmask_ref, qdec_ref, kdec_ref, gn_ref, s0_ref,
                      o_ref, sout_ref, s_ref, *, nh, dk, dv):
    j = pl.program_id(1)

    @pl.when(j == 0)
    def _():
        s_ref[...] = s0_ref[...]

    for hd in range(nh):
        qh = q_ref[:, hd * dk:(hd + 1) * dk]
        kh = k_ref[:, hd * dk:(hd + 1) * dk]
        vh = v_ref[:, hd * dv:(hd + 1) * dv]
        state = s_ref[hd]
        scores = _dot_nt(qh, kh) * dmask_ref[hd]
        o = _dot(scores.astype(BF16), vh) + _dot(qh, state.astype(BF16)) * qdec_ref[hd]
        kd = (kh.astype(F32) * kdec_ref[hd]).astype(BF16)
        s_ref[hd] = sdec_ref[hd] * state + _dot_tn(kd, vh)
        mu = jnp.mean(o, axis=-1, keepdims=True)
        oc = o - mu
        var = jnp.mean(oc * oc, axis=-1, keepdims=True)
        on = oc * lax.rsqrt(var + EPS) * gn_ref[:, hd * dv:(hd + 1) * dv]
        o_ref[:, hd * dv:(hd + 1) * dv] = (sg_ref[:, hd * dv:(hd + 1) * dv].astype(F32) * on).astype(BF16)

    @pl.when(j == pl.num_programs(1) - 1)
    def _():
        sout_ref[...] = s_ref[...]


def _retention(q, k, v, sg, s0, gn_g, batch, seq_len, chunk):
    nh, dk, dv = s0.shape[1:]
    rv = nh * dv
    t = _row_tile(seq_len, max(chunk, 256))
    assert t % chunk == 0
    nblk = seq_len // t
    dmask, qdec, kdec, sdec = _ret_decay_tables(nh, t, chunk)
    row = lambda w: pl.BlockSpec((t, w), lambda b, j: (b * nblk + j, 0))
    st = pl.BlockSpec((None, nh, dk, dv), lambda b, j: (b, 0, 0, 0))
    return pl.pallas_call(
        functools.partial(_retention_kernel, nh=nh, dk=dk, dv=dv),
        out_shape=(jax.ShapeDtypeStruct((batch * seq_len, rv), BF16),
                   jax.ShapeDtypeStruct((batch, nh, dk, dv), F32)),
        grid=(batch, nblk),
        in_specs=[
            pl.BlockSpec(memory_space=pltpu.SMEM),
            row(nh * dk), row(nh * dk), row(rv), row(rv),
            _resident((nh, t, t)), _resident((nh, t, 1)), _resident((nh, t, 1)), _resident((1, rv)),
            st,
        ],
        out_specs=(row(rv), st),
        scratch_shapes=[pltpu.VMEM((nh, dk, dv), F32)],
        compiler_params=_cparams(("parallel", "arbitrary")),
        name="retention",
    )(sdec, q, k, v, sg, dmask, qdec, kdec, gn_g.reshape(1, rv), s0)


def _lane_iota(rows):
    return lax.broadcasted_iota(jnp.int32, (rows, LANES), 1)


def _head_lanes(x, hd):
    t = x[:, (hd // 2) * LANES:(hd // 2 + 1) * LANES]
    return pltpu.roll(t, HALF_LANES, axis=1) if hd % 2 else t


def _tri_cumsum(tri, x):
    hi, mid, lo = _split3(x)
    return _dot(tri, hi.astype(BF16)) + _dot(tri, mid.astype(BF16)) + _dot(tri, lo.astype(BF16))


def _pack_keys(k, c, ka_ref, nh):
    rows = k.shape[0]
    lane = _lane_iota(rows)
    hi, mid, lo = _split3(c * LOG2E)
    extra = jnp.where(lane < AUG_QPIECE0 + N_PIECES, 1.0, 0.0)
    for p, piece in reversed(list(enumerate((hi, mid, lo)))):
        lo_lane = AUG_PIECE0 + p * nh
        extra = jnp.where(lane < lo_lane + nh, -pltpu.roll(piece, lo_lane, axis=1), extra)
    for hd in range(nh):
        ka_ref[hd] = jnp.where(lane < HALF_LANES, _head_lanes(k, hd), extra).astype(BF16)


def _pack_values(v, vt_ref, nh, tk):
    rows = v.shape[0]
    vt = v.T
    sub = lax.broadcasted_iota(jnp.int32, (VT_ROWS - HALF_LANES, rows), 0)
    ones_row = jnp.where(sub == 0, 1.0, 0.0)
    for hd in range(nh):
        t = jnp.concatenate([vt[hd * HALF_LANES:(hd + 1) * HALF_LANES, :], ones_row], axis=0).astype(BF16)
        for s in range(rows // tk):
            vt_ref[hd, s] = t[:, s * tk:(s + 1) * tk]


def _fox_kv_kernel(x_ref, g_ref, wk_ref, wv_ref, wf_ref, bf_ref, tri_ref, k_ref, v_ref, logf_ref, *rest,
                   nh, tk, packed):
    tm = x_ref.shape[0]
    ts = tri_ref.shape[0]
    if packed:
        ka_ref, vt_ref, ct_ref, carry_ref = rest

        @pl.when(pl.program_id(1) == 0)
        def _():
            carry_ref[...] = jnp.zeros_like(carry_ref)

    def project(r):
        rows = pl.ds(r * ts, ts)
        xn = _rmsnorm(x_ref[rows, :], g_ref[...]).astype(BF16)
        k = _dot(xn, wk_ref[...])
        v = _dot(xn, wv_ref[...])
        z = _dot(xn, wf_ref[...]) + bf_ref[...]
        return k, v, jnp.minimum(z, 0.0) - jnp.log1p(jnp.exp(-jnp.abs(z)))

    def emit(r, k, v, logf):
        logf_ref[pl.ds(r * ts, ts), :] = logf
        for src, dst in ((k, k_ref), (v, v_ref)):
            for hd in range(nh):
                dst[pl.ds(r * ts * nh + hd, ts, stride=nh), :] = _head_lanes(src, hd)[:, :HALF_LANES]
        if packed:
            c = _tri_cumsum(tri_ref[...], jnp.where(_lane_iota(ts) < nh, logf, 0.0)) + carry_ref[...]
            carry_ref[...] = c[ts - 1:ts, :]
            ct_ref[:, pl.ds(r * ts, ts)] = c.T[:nh, :]
            _pack_keys(k, c, ka_ref.at[:, pl.ds(r * ts, ts), :], nh)
            _pack_values(v, vt_ref.at[:, pl.ds(r * (ts // tk), ts // tk)], nh, tk)

    pending = project(0)
    for r in range(tm // ts):
        done = pending
        if r + 1 < tm // ts:
            pending = project(r + 1)
        emit(r, *done)


def _fox_kv(h, g, w_kvf, b_f, width, nh, batch, seq_len, tk=None):
    n, d = h.shape
    packed = tk is not None
    tm = _row_tile(seq_len if packed else n, 512)
    grid = (batch, seq_len // tm) if packed else (n // tm, 1)
    nblk = grid[1]
    assert width == nh * HALF_LANES and N_PIECES * nh <= AUG_QPIECE0 - AUG_PIECE0
    assert nh & (nh - 1) == 0 and AUG_PIECE0 % nh == 0
    wf = jnp.zeros((d, LANES), BF16).at[:, :nh].set(w_kvf[:, 2 * width:].astype(BF16))
    bf = jnp.zeros((1, LANES), F32).at[0, :nh].set(b_f)
    ts = min(tm, 256)
    assert tm % ts == 0
    tri = (jnp.arange(ts)[:, None] >= jnp.arange(ts)[None, :]).astype(BF16)
    row = lambda r, w: pl.BlockSpec((r, w), lambda i, j: (i * nblk + j, 0))
    out_shape = [jax.ShapeDtypeStruct((n * nh, HALF_LANES), F32), jax.ShapeDtypeStruct((n * nh, HALF_LANES), F32),
                 jax.ShapeDtypeStruct((n, LANES), F32)]
    out_specs = [row(tm * nh, HALF_LANES), row(tm * nh, HALF_LANES), row(tm, LANES)]
    scratch = []
    if packed:
        assert ts % tk == 0
        out_shape += [jax.ShapeDtypeStruct((batch, nh, seq_len, LANES), BF16),
                      jax.ShapeDtypeStruct((batch, nh, seq_len // tk, VT_ROWS, tk), BF16),
                      jax.ShapeDtypeStruct((batch, nh, seq_len), F32)]
        out_specs += [pl.BlockSpec((None, nh, tm, LANES), lambda i, j: (i, 0, j, 0)),
                      pl.BlockSpec((None, nh, tm // tk, VT_ROWS, tk), lambda i, j: (i, 0, j, 0, 0)),
                      pl.BlockSpec((None, nh, tm), lambda i, j: (i, 0, j))]
        scratch = [pltpu.VMEM((1, LANES), F32)]
    return pl.pallas_call(
        functools.partial(_fox_kv_kernel, nh=nh, tk=tk, packed=packed),
        out_shape=tuple(out_shape),
        grid=grid,
        in_specs=[row(tm, d), _resident((1, d)), _resident((d, width)), _resident((d, width)),
                  _resident((d, LANES)), _resident((1, LANES)), _resident((ts, ts))],
        out_specs=tuple(out_specs),
        scratch_shapes=scratch,
        compiler_params=_cparams(("parallel", "arbitrary")),
        name="fox_kv",
    )(h, g.reshape(1, d), w_kvf[:, :width].astype(BF16), w_kvf[:, width:2 * width].astype(BF16), wf, bf, tri)


def _cumsum_kernel(x_ref, tri_ref, o_ref, *, tc, nblk):
    tri = tri_ref[...]

    def body(i, carry):
        r0 = pl.multiple_of(i * tc, tc)
        c = _tri_cumsum(tri, x_ref[pl.ds(r0, tc), :]) + carry
        o_ref[pl.ds(r0, tc), :] = c
        return c[tc - 1:tc, :]

    lax.fori_loop(0, nblk, body, jnp.zeros((1, x_ref.shape[-1]), F32))


def _cumsum_rows(x):
    b, n, w = x.shape
    tc = 128 if n % 128 == 0 else 64
    assert n % tc == 0
    tri = (jnp.arange(tc)[:, None] >= jnp.arange(tc)[None, :]).astype(BF16)
    blk = pl.BlockSpec((None, n, w), lambda i: (i, 0, 0))
    return pl.pallas_call(
        functools.partial(_cumsum_kernel, tc=tc, nblk=n // tc),
        out_shape=jax.ShapeDtypeStruct((b, n, w), F32),
        grid=(b,),
        in_specs=[blk, _resident((tc, tc))],
        out_specs=blk,
        compiler_params=_cparams(("parallel",)),
        name="cumsum",
    )(x, tri)


def _kpack_kernel(ko_ref, vo_ref, kn_ref, vn_ref, c_ref, ka_ref, vt_ref, *, nh):
    fresh = pl.program_id(1) == pl.num_programs(1) - 1
    rows = ko_ref.shape[0]
    _pack_keys(jnp.where(fresh, kn_ref[...], ko_ref[...]), c_ref[...], ka_ref, nh)
    _pack_values(jnp.where(fresh, vn_ref[...], vo_ref[...]), vt_ref, nh, rows)


def _kpack(k_old, v_old, k_new, v_new, c, nh):
    b, n_old, width = k_old.shape
    tm = k_new.shape[1]
    assert n_old % tm == 0 and tm % LANES == 0
    n_old_tiles = n_old // tm
    n = n_old + tm
    old = pl.BlockSpec((None, tm, width), lambda i, j: (i, jnp.minimum(j, n_old_tiles - 1), 0))
    new = pl.BlockSpec((None, tm, width), lambda i, j: (i, 0, 0))
    return pl.pallas_call(
        functools.partial(_kpack_kernel, nh=nh),
        out_shape=(jax.ShapeDtypeStruct((b, nh, n, LANES), BF16),
                   jax.ShapeDtypeStruct((b, nh, 1, VT_ROWS, n), BF16)),
        grid=(b, n_old_tiles + 1),
        in_specs=[old, old, new, new, pl.BlockSpec((None, tm, LANES), lambda i, j: (i, j, 0))],
        out_specs=(pl.BlockSpec((None, nh, tm, LANES), lambda i, j: (i, 0, j, 0)),
                   pl.BlockSpec((None, nh, 1, VT_ROWS, tm), lambda i, j: (i, 0, 0, 0, j))),
        compiler_params=_cparams(("parallel", "arbitrary")),
        name="kpack",
    )(k_old, v_old, k_new, v_new, c)


def _fox_q_kernel(x_ref, g_ref, wt_ref, ct_ref, qa_ref, *, nh):
    tm = x_ref.shape[0]
    xn = _rmsnorm(x_ref[...], g_ref[...]).astype(BF16)
    qt = _dot_nt(wt_ref[...], xn) * (HALF_LANES ** -0.5 * LOG2E)
    hi, mid, lo = _split3(ct_ref[...] * LOG2E)
    sub = lax.broadcasted_iota(jnp.int32, (8, tm), 0)
    rowp = lax.broadcasted_iota(jnp.int32, (N_PIECES * nh, tm), 0)
    tail = jnp.zeros((LANES - AUG_QPIECE0 - 8, tm), F32)
    for hd in range(nh):
        onehot = jnp.where((rowp & (nh - 1)) == hd, 1.0, 0.0)
        own = [jnp.broadcast_to(a[hd:hd + 1, :], (8, tm)) for a in (hi, mid, lo)]
        pieces = jnp.where(sub == 0, own[0], jnp.where(sub == 1, own[1], jnp.where(sub == 2, own[2], 0.0)))
        tile = jnp.concatenate([qt[hd * HALF_LANES:(hd + 1) * HALF_LANES, :], onehot, pieces, tail], axis=0)
        qa_ref[hd] = tile.astype(BF16)


def _fox_q(h, g, w_q, cq_t, batch, seq_len, nh):
    n, d = h.shape
    tm = _row_tile(seq_len, 1024)
    nblk = seq_len // tm
    return pl.pallas_call(
        functools.partial(_fox_q_kernel, nh=nh),
        out_shape=jax.ShapeDtypeStruct((batch, nh, LANES, seq_len), BF16),
        grid=(batch, nblk),
        in_specs=[
            pl.BlockSpec((tm, d), lambda b, j: (b * nblk + j, 0)),
            _resident((1, d)),
            _resident((nh * HALF_LANES, d)),
            pl.BlockSpec((None, nh, tm), lambda b, j: (b, 0, j)),
        ],
        out_specs=pl.BlockSpec((None, nh, LANES, tm), lambda b, j: (b, 0, 0, j)),
        compiler_params=_cparams(("parallel", "parallel")),
        name="fox_q",
    )(h, g.reshape(1, d), w_q.T.astype(BF16), cq_t)


def _fox_attn_kernel(q_ref, k_ref, v_ref, o_ref, s0_ref, s1_ref, acc_ref, m_ref, *, hg, tq, tk, q_off,
                     single_block):
    i = pl.program_id(2)
    q0 = q_off + i * tq
    n_full, n_mask = (0, 1) if single_block else (q0 // tk, tq // tk)
    acc_ref[...] = jnp.zeros_like(acc_ref)
    m_ref[...] = jnp.full_like(m_ref, -jnp.inf)
    slots = (s0_ref, s1_ref)
    cw = min(tq, 256)

    def scores_head(hh, j, slot, c0=0, c1=tq):
        r0 = pl.multiple_of(j * tk, tk)
        slots[slot][hh, :, c0:c1] = _dot(k_ref[hh, pl.ds(r0, tk), :], q_ref[hh, :, c0:c1])

    def absorb_head(hh, j, slot, c0=0, c1=tq, diag=None):
        w = c1 - c0
        s = slots[slot][hh, :, c0:c1]
        if diag is not None:
            kpos = lax.broadcasted_iota(jnp.int32, (tk, w), 0)
            qpos = lax.broadcasted_iota(jnp.int32, (tk, w), 1)
            s = jnp.where(qpos + diag >= kpos, s, -jnp.inf)
        m_prev = m_ref[hh, :, c0:c1]
        m_new = jnp.maximum(m_prev, jnp.max(s, axis=0, keepdims=True))
        alpha = jnp.exp2(m_prev - m_new)
        p = jnp.exp2(s - m_new).astype(BF16)
        acc_ref[hh, :, c0:c1] = alpha * acc_ref[hh, :, c0:c1] + _dot(v_ref[hh, j], p)
        m_ref[hh, :, c0:c1] = m_new

    def scores(j, slot, c0=0):
        for hh in range(hg):
            scores_head(hh, j, slot, c0)

    def absorb(j, slot, c0=0, c1=tq, diag=None):
        for hh in range(hg):
            absorb_head(hh, j, slot, c0, c1, diag)

    def stage(j, slot):
        for hh in range(hg):
            for c0 in range(0, tq, cw):
                scores_head(hh, j + 1, 1 - slot, c0, c0 + cw)
                absorb_head(hh, j, slot, c0, c0 + cw)

    scores(0, 0)

    def quad(jj, carry):
        for u in range(4):
            stage(4 * jj + u, u % 2)
        return carry

    lax.fori_loop(0, n_full // 4, quad, 0)

    @pl.when(n_full % 4 == 2)
    def _():
        stage(n_full - 2, 0)
        stage(n_full - 1, 1)
    for t in range(n_mask):
        lo = 0 if single_block else t * tk
        hi = min(lo + tk, tq)
        for hh in range(hg):
            if t + 1 < n_mask:
                scores_head(hh, n_full + t + 1, (t + 1) % 2, hi)
            absorb_head(hh, n_full + t, t % 2, lo, hi, diag=q_off if single_block else 0)
            if hi < tq:
                absorb_head(hh, n_full + t, t % 2, hi, tq)
    for g in range(hg // 2):
        halves = []
        for hh in (2 * g, 2 * g + 1):
            a = acc_ref[hh]
            halves.append(a[:HALF_LANES, :] / a[HALF_LANES:HALF_LANES + 1, :])
        o_ref[:, g * LANES:(g + 1) * LANES] = jnp.concatenate(halves, axis=0).T.astype(BF16)


def _fox_attn(qa, ka, vt, q_off):
    b, nh, _, seq_len = qa.shape
    n_keys = ka.shape[2]
    tk = vt.shape[-1]
    hg = 8
    tq = min(seq_len, 512)
    assert seq_len % tq == 0 and tq % LANES == 0 and nh % hg == 0 and q_off + seq_len <= n_keys
    nq = seq_len // tq
    single_block = n_keys == tk
    assert (single_block and nq == 1) or (tq % (2 * tk) == 0 and q_off % (2 * tk) == 0)
    return pl.pallas_call(
        functools.partial(_fox_attn_kernel, hg=hg, tq=tq, tk=tk, q_off=q_off, single_block=single_block),
        out_shape=jax.ShapeDtypeStruct((b * seq_len, nh * HALF_LANES), BF16),
        grid=(b, nh // hg, nq),
        in_specs=[
            pl.BlockSpec((None, hg, LANES, tq), lambda bi, g, i: (bi, g, 0, i)),
            pl.BlockSpec((None, hg, n_keys, LANES), lambda bi, g, i: (bi, g, 0, 0)),
            pl.BlockSpec((None, hg, n_keys // tk, VT_ROWS, tk), lambda bi, g, i: (bi, g, 0, 0, 0)),
        ],
        out_specs=pl.BlockSpec((tq, hg * HALF_LANES), lambda bi, g, i: (bi * nq + i, g)),
        scratch_shapes=[pltpu.VMEM((hg, tk, tq), F32), pltpu.VMEM((hg, tk, tq), F32),
                        pltpu.VMEM((hg, VT_ROWS, tq), F32), pltpu.VMEM((hg, 1, tq), F32)],
        compiler_params=_cparams(("parallel", "parallel", "arbitrary"),
                                 ),
        name="fox_attn",
    )(qa, ka, vt)


def _trunk(x, pos0, chunk, s0, past, p):
    batch, seq_len, d = x.shape
    nh_ret, dk, dv = s0.shape[2:]
    rv = nh_ret * dv
    nh_fox = p['fox_b_f'].shape[0]
    width = p['fox_w_q'].shape[-1]
    pos = pos0 + jnp.arange(seq_len)
    h = x.reshape(batch * seq_len, d)

    h = _ffn(h, p['ffn1_g'][0], p['ffn1_w_in'][0], p['ffn1_w_out'][0])
    q, k, v, sg = _ret_proj(h, p['mix_g'][0], p['ret_w_in'][0], pos, seq_len, nh_ret, rv)
    o, s_fin = _retention(q, k, v, sg, s0[0], p['ret_gn_g'][0], batch, seq_len, chunk)
    h = _proj_res(o, p['ret_w_out'][0], h)
    h = _ffn(h, p['ffn2_g'][0], p['ffn2_w_in'][0], p['ffn2_w_out'][0])

    head_dim = width // nh_fox
    pad_q = -seq_len % LANES
    if past is None:
        assert pad_q == 0
        q_off = 0
        k_new, v_new, logf_pad, ka, vt, cq_t = _fox_kv(h, p['kv_g'], p['fox_w_kvf'], p['fox_b_f'], width, nh_fox,
                                                       batch, seq_len, tk=256)
    else:
        k_new, v_new, logf_pad = _fox_kv(h, p['kv_g'], p['fox_w_kvf'], p['fox_b_f'], width, nh_fox, batch, seq_len)
    k_new = k_new.reshape(batch, seq_len, nh_fox, head_dim)
    v_new = v_new.reshape(batch, seq_len, nh_fox, head_dim)
    logf_new = logf_pad.reshape(batch, seq_len, LANES)[:, :, :nh_fox]
    if past is not None:
        past_len = past[0].shape[1]
        q_off = past_len
        tile = 256 if past_len % 256 == 0 else LANES
        assert seq_len + pad_q <= tile
        pad_k = tile - seq_len
        k_fresh, v_fresh = (jnp.pad(t.reshape(batch, seq_len, width), ((0, 0), (0, pad_k), (0, 0)))
                            for t in (k_new, v_new))
        logf_all = jnp.concatenate([past[2], logf_new], axis=1)
        cumf = _cumsum_rows(jnp.pad(logf_all, ((0, 0), (0, pad_k), (0, LANES - nh_fox))))
        ka, vt = _kpack(past[0].reshape(batch, past_len, width), past[1].reshape(batch, past_len, width),
                        k_fresh, v_fresh, cumf, nh_fox)
        cq_t = jnp.swapaxes(cumf[:, q_off:q_off + seq_len, :nh_fox], 1, 2)

    h = _ffn(h, p['ffn1_g'][1], p['ffn1_w_in'][1], p['ffn1_w_out'][1])
    qa = _fox_q(h, p['mix_g'][1], p['fox_w_q'][0], cq_t, batch, seq_len, nh_fox)
    if pad_q:
        qa = jnp.pad(qa, ((0, 0), (0, 0), (0, 0), (0, pad_q)))
    o = _fox_attn(qa, ka, vt, q_off)
    if pad_q:
        o = o.reshape(batch, seq_len + pad_q, width)[:, :seq_len].reshape(batch * seq_len, width)
    h = _proj_res(o, p['fox_w_out'][0], h)
    y = _ffn(h, p['ffn2_g'][1], p['ffn2_w_in'][1], p['ffn2_w_out'][1], final_g=p['final_g'])

    return (y.reshape(batch, seq_len, d), s_fin[None], k_new, v_new, logf_new)


def kernel(x_prompt, x_sample, state_ret, cache_k, cache_v, cache_logf, ffn1_g, ffn1_w_in, ffn1_w_out, mix_g,
           ffn2_g, ffn2_w_in, ffn2_w_out, ret_w_in, ret_gn_g, ret_w_out, kv_g, fox_w_kvf, fox_b_f, fox_w_q,
           fox_w_out, final_g):
    p = {'ffn1_g': ffn1_g, 'ffn1_w_in': ffn1_w_in, 'ffn1_w_out': ffn1_w_out, 'mix_g': mix_g,
         'ffn2_g': ffn2_g, 'ffn2_w_in': ffn2_w_in, 'ffn2_w_out': ffn2_w_out,
         'ret_w_in': ret_w_in, 'ret_gn_g': ret_gn_g, 'ret_w_out': ret_w_out,
         'kv_g': kv_g, 'fox_w_kvf': fox_w_kvf, 'fox_b_f': fox_b_f,
         'fox_w_q': fox_w_q, 'fox_w_out': fox_w_out, 'final_g': final_g}
    assert state_ret.shape[0] == 1 and fox_w_q.shape[0] == 1
    for name in ('ffn1_w_in', 'ffn1_w_out', 'ffn2_w_in', 'ffn2_w_out', 'ret_w_in', 'ret_w_out', 'fox_w_kvf',
                 'fox_w_q', 'fox_w_out'):
        p[name] = _to_bf16(p[name])
    s0_prompt = jnp.zeros((1, x_prompt.shape[0]) + state_ret.shape[2:], F32)
    y_p, s_p, k_p, v_p, f_p = _trunk(x_prompt, 0, RET_CHUNK, s0_prompt, None, p)
    y_s, s_s, k_s, v_s, f_s = _trunk(x_sample, cache_k.shape[1], x_sample.shape[1], state_ret,
                                     (cache_k, cache_v, cache_logf), p)
    return (y_p, y_s, s_p, k_p, v_p, f_p, s_s, k_s, v_s, f_s)
```

```python
import functools

import jax
import jax.numpy as jnp
from jax import lax
from jax.experimental import pallas as pl
from jax.experimental.pallas import tpu as pltpu

F32 = jnp.float32
BF16 = jnp.bfloat16

EPS = 1e-6
ROPE_BASE = 10000.0
RET_CHUNK = 64
LANES = 128
HALF_LANES = LANES // 2
V7X_VMEM_LIMIT = 56 * 1024 * 1024
CAST_TILE_BYTES = 6 * 1024 * 1024

AUG_PIECE0 = 64
AUG_QPIECE0 = 112
N_PIECES = 3
VT_ROWS = 80
LOG2E = 1.4426950408889634


def _cparams(sem, flags=None):
    return pltpu.CompilerParams(dimension_semantics=sem, vmem_limit_bytes=V7X_VMEM_LIMIT, flags=flags)


def _resident(shape):
    nd = len(shape)
    return pl.BlockSpec(shape, lambda *_: (0,) * nd, pipeline_mode=pl.Buffered(1))


def _rmsnorm(x, g):
    ms = jnp.mean(x * x, axis=-1, keepdims=True)
    return x * lax.rsqrt(ms + EPS) * g


def _silu(x):
    return x * jax.nn.sigmoid(x)


def _dot(a, b):
    return jnp.dot(a, b, preferred_element_type=F32)


def _dot_nt(a, b):
    return lax.dot_general(a, b, (((1,), (1,)), ((), ())), preferred_element_type=F32)


def _dot_tn(a, b):
    return lax.dot_general(a, b, (((0,), (0,)), ((), ())), preferred_element_type=F32)


def _split3(c):
    hi = c.astype(BF16).astype(F32)
    r = c - hi
    mid = r.astype(BF16).astype(F32)
    lo = (r - mid).astype(BF16).astype(F32)
    return hi, mid, lo


def _row_tile(n, want):
    t = min(n, want)
    assert n % t == 0 and t % 8 == 0, (n, t)
    return t


def _cast_kernel(x_ref, o_ref):
    o_ref[...] = x_ref[...].astype(o_ref.dtype)


def _to_bf16(w):
    w2 = w.reshape(-1, w.shape[-1])
    r, c = w2.shape
    max_rows = max(16, CAST_TILE_BYTES // (4 * c))
    tr = max(t for t in range(16, r + 1, 16) if r % t == 0 and t <= max_rows)
    out = pl.pallas_call(
        _cast_kernel,
        out_shape=jax.ShapeDtypeStruct((r, c), BF16),
        grid=(r // tr,),
        in_specs=[pl.BlockSpec((tr, c), lambda i: (i, 0))],
        out_specs=pl.BlockSpec((tr, c), lambda i: (i, 0)),
        compiler_params=_cparams(("parallel",)),
        name="cast_bf16",
    )(w2)
    return out.reshape(w.shape)


def _ffn_kernel(x_ref, g_ref, win_ref, wout_ref, fg_ref, o_ref, act_ref, *, d_ff, ck, final_norm):
    x = x_ref[...]
    xn = _rmsnorm(x, g_ref[...]).astype(BF16)
    for c in range(d_ff // ck):
        gate = _dot(xn, win_ref[:, c * ck:(c + 1) * ck])
        up = _dot(xn, win_ref[:, d_ff + c * ck:d_ff + (c + 1) * ck])
        act_ref[:, c * ck:(c + 1) * ck] = (_silu(gate) * up).astype(BF16)
    y = x + 0.5 * _dot(act_ref[...], wout_ref[...])
    if final_norm:
        y = _rmsnorm(y, fg_ref[...])
    o_ref[...] = y


def _ffn(h, g, w_in, w_out, final_g=None):
    n, d = h.shape
    d_ff = w_out.shape[0]
    tm = _row_tile(n, 1024)
    ck = 256
    assert d_ff % ck == 0
    fg = jnp.ones((d,), F32) if final_g is None else final_g
    return pl.pallas_call(
        functools.partial(_ffn_kernel, d_ff=d_ff, ck=ck, final_norm=final_g is not None),
        out_shape=jax.ShapeDtypeStruct((n, d), F32),
        grid=(n // tm,),
        in_specs=[
            pl.BlockSpec((tm, d), lambda i: (i, 0)),
            _resident((1, d)),
            _resident((d, 2 * d_ff)),
            _resident((d_ff, d)),
            _resident((1, d)),
        ],
        out_specs=pl.BlockSpec((tm, d), lambda i: (i, 0)),
        scratch_shapes=[pltpu.VMEM((tm, d_ff), BF16)],
        compiler_params=_cparams(("parallel",)),
        name="ffn",
    )(h, g.reshape(1, d), w_in.astype(BF16), w_out.astype(BF16), fg.reshape(1, d))


def _proj_res_kernel(a_ref, w_ref, r_ref, o_ref):
    o_ref[...] = r_ref[...] + _dot(a_ref[...], w_ref[...])


def _proj_res(a, w, res):
    n, k = a.shape
    d = w.shape[1]
    tm = _row_tile(n, 1024)
    return pl.pallas_call(
        _proj_res_kernel,
        out_shape=jax.ShapeDtypeStruct((n, d), F32),
        grid=(n // tm,),
        in_specs=[
            pl.BlockSpec((tm, k), lambda i: (i, 0)),
            _resident((k, d)),
            pl.BlockSpec((tm, d), lambda i: (i, 0)),
        ],
        out_specs=pl.BlockSpec((tm, d), lambda i: (i, 0)),
        compiler_params=_cparams(("parallel",)),
        name="proj_res",
    )(a, w.astype(BF16), res)


def _rope_tables(pos, half):
    inv = ROPE_BASE ** (-jnp.arange(half, dtype=F32) / half)
    ang = pos.astype(F32)[:, None] * inv[None, :]
    return jnp.cos(ang), jnp.sin(ang)


def _ret_proj_kernel(x_ref, g_ref, w_ref, cos_ref, sin_ref, q_ref, k_ref, v_ref, sg_ref, *, d, nh, rv):
    dk = d // nh
    half = dk // 2
    tm = x_ref.shape[0]
    ts = min(tm, 256)
    norm = lambda r: _rmsnorm(x_ref[pl.ds(r * ts, ts), :], g_ref[...]).astype(BF16)
    xn = norm(0)
    for r in range(tm // ts):
        rows = pl.ds(r * ts, ts)
        cos = cos_ref[rows, :]
        sin = sin_ref[rows, :]
        sg_ref[rows, :] = _silu(_dot(xn, w_ref[:, 2 * d + rv:2 * d + 2 * rv])).astype(BF16)
        xn_next = norm(r + 1) if r + 1 < tm // ts else None
        for off, out_ref, scale in ((0, q_ref, dk ** -0.5), (d, k_ref, 1.0)):
            t = _dot(xn, w_ref[:, off:off + d])
            for hd in range(nh):
                x1 = t[:, hd * dk:hd * dk + half]
                x2 = t[:, hd * dk + half:(hd + 1) * dk]
                out_ref[rows, hd * dk:hd * dk + half] = ((x1 * cos - x2 * sin) * scale).astype(BF16)
                out_ref[rows, hd * dk + half:(hd + 1) * dk] = ((x1 * sin + x2 * cos) * scale).astype(BF16)
        v_ref[rows, :] = _dot(xn, w_ref[:, 2 * d:2 * d + rv]).astype(BF16)
        xn = xn_next


def _ret_proj(h, g, w_in, pos, seq_len, nh, rv):
    n, d = h.shape
    half = d // nh // 2
    tm = _row_tile(n, 512)
    cos, sin = _rope_tables(pos, half)
    if seq_len % tm == 0:
        period = seq_len // tm
    else:
        assert tm % seq_len == 0
        cos, sin = (jnp.tile(t, (tm // seq_len, 1)) for t in (cos, sin))
        period = 1
    tab = pl.BlockSpec((tm, half), lambda i: (i % period, 0))
    row = lambda w: pl.BlockSpec((tm, w), lambda i: (i, 0))
    return pl.pallas_call(
        functools.partial(_ret_proj_kernel, d=d, nh=nh, rv=rv),
        out_shape=(jax.ShapeDtypeStruct((n, d), BF16), jax.ShapeDtypeStruct((n, d), BF16),
                   jax.ShapeDtypeStruct((n, rv), BF16), jax.ShapeDtypeStruct((n, rv), BF16)),
        grid=(n // tm,),
        in_specs=[row(d), _resident((1, d)), _resident((d, 2 * d + 2 * rv)), tab, tab],
        out_specs=(row(d), row(d), row(rv), row(rv)),
        compiler_params=_cparams(("parallel",)),
        name="ret_proj",
    )(h, g.reshape(1, d), w_in.astype(BF16), cos, sin)


def _ret_decay_tables(nh, t, chunk):
    log_gamma = jnp.log1p(-jnp.exp2(-5.0 - jnp.arange(nh, dtype=F32)))
    pos = jnp.arange(t, dtype=F32)
    dist = jnp.abs(pos[:, None] - pos[None, :])
    cid = jnp.arange(t) // chunk
    visible = (cid[None, :] <= cid[:, None]).astype(F32)
    dmask = jnp.exp(log_gamma[:, None, None] * dist) * visible[None]
    qdec = jnp.exp(log_gamma[:, None] * (pos[None, :] + 1.0))[:, :, None]
    kdec = jnp.exp(log_gamma[:, None] * (t - 1.0 - pos[None, :]))[:, :, None]
    sdec = jnp.exp(log_gamma * t)
    return dmask, qdec, kdec, sdec


def _retention_kernel(sdec_ref, q_ref, k_ref, v_ref, sg_ref, dmask_ref, qdec_ref, kdec_ref, gn_ref, s0_ref,
                      o_ref, sout_ref, s_ref, *, nh, dk, dv):
    j = pl.program_id(1)

    @pl.when(j == 0)
    def _():
        s_ref[...] = s0_ref[...]

    for hd in range(nh):
        qh = q_ref[:, hd * dk:(hd + 1) * dk]
        kh = k_ref[:, hd * dk:(hd + 1) * dk]
        vh = v_ref[:, hd * dv:(hd + 1) * dv]
        state = s_ref[hd]
        scores = _dot_nt(qh, kh) * dmask_ref[hd]
        o = _dot(scores.astype(BF16), vh) + _dot(qh, state.astype(BF16)) * qdec_ref[hd]
        kd = (kh.astype(F32) * kdec_ref[hd]).astype(BF16)
        s_ref[hd] = sdec_ref[hd] * state + _dot_tn(kd, vh)
        mu = jnp.mean(o, axis=-1, keepdims=True)
        oc = o - mu
        var = jnp.mean(oc * oc, axis=-1, keepdims=True)
        on = oc * lax.rsqrt(var + EPS) * gn_ref[:, hd * dv:(hd + 1) * dv]
        o_ref[:, hd * dv:(hd + 1) * dv] = (sg_ref[:, hd * dv:(hd + 1) * dv].astype(F32) * on).astype(BF16)

    @pl.when(j == pl.num_programs(1) - 1)
    def _():
        sout_ref[...] = s_ref[...]


def _retention(q, k, v, sg, s0, gn_g, batch, seq_len, chunk):
    nh, dk, dv = s0.shape[1:]
    rv = nh * dv
    t = _row_tile(seq_len, max(chunk, 256))
    assert t % chunk == 0
    nblk = seq_len // t
    dmask, qdec, kdec, sdec = _ret_decay_tables(nh, t, chunk)
    row = lambda w: pl.BlockSpec((t, w), lambda b, j: (b * nblk + j, 0))
    st = pl.BlockSpec((None, nh, dk, dv), lambda b, j: (b, 0, 0, 0))
    return pl.pallas_call(
        functools.partial(_retention_kernel, nh=nh, dk=dk, dv=dv),
        out_shape=(jax.ShapeDtypeStruct((batch * seq_len, rv), BF16),
                   jax.ShapeDtypeStruct((batch, nh, dk, dv), F32)),
        grid=(batch, nblk),
        in_specs=[
            pl.BlockSpec(memory_space=pltpu.SMEM),
            row(nh * dk), row(nh * dk), row(rv), row(rv),
            _resident((nh, t, t)), _resident((nh, t, 1)), _resident((nh, t, 1)), _resident((1, rv)),
            st,
        ],
        out_specs=(row(rv), st),
        scratch_shapes=[pltpu.VMEM((nh, dk, dv), F32)],
        compiler_params=_cparams(("parallel", "arbitrary")),
        name="retention",
    )(sdec, q, k, v, sg, dmask, qdec, kdec, gn_g.reshape(1, rv), s0)


def _lane_iota(rows):
    return lax.broadcasted_iota(jnp.int32, (rows, LANES), 1)


def _head_lanes(x, hd):
    t = x[:, (hd // 2) * LANES:(hd // 2 + 1) * LANES]
    return pltpu.roll(t, HALF_LANES, axis=1) if hd % 2 else t


def _tri_cumsum(tri, x):
    hi, mid, lo = _split3(x)
    return _dot(tri, hi.astype(BF16)) + _dot(tri, mid.astype(BF16)) + _dot(tri, lo.astype(BF16))


def _pack_keys(k, c, ka_ref, nh):
    rows = k.shape[0]
    lane = _lane_iota(rows)
    hi, mid, lo = _split3(c * LOG2E)
    extra = jnp.where(lane < AUG_QPIECE0 + N_PIECES, 1.0, 0.0)
    for p, piece in reversed(list(enumerate((hi, mid, lo)))):
        lo_lane = AUG_PIECE0 + p * nh
        extra = jnp.where(lane < lo_lane + nh, -pltpu.roll(piece, lo_lane, axis=1), extra)
    for hd in range(nh):
        ka_ref[hd] = jnp.where(lane < HALF_LANES, _head_lanes(k, hd), extra).astype(BF16)


def _pack_values(v, vt_ref, nh, tk):
    rows = v.shape[0]
    vt = v.T
    sub = lax.broadcasted_iota(jnp.int32, (VT_ROWS - HALF_LANES, rows), 0)
    ones_row = jnp.where(sub == 0, 1.0, 0.0)
    for hd in range(nh):
        t = jnp.concatenate([vt[hd * HALF_LANES:(hd + 1) * HALF_LANES, :], ones_row], axis=0).astype(BF16)
        for s in range(rows // tk):
            vt_ref[hd, s] = t[:, s * tk:(s + 1) * tk]


def _fox_kv_kernel(x_ref, g_ref, wk_ref, wv_ref, wf_ref, bf_ref, tri_ref, k_ref, v_ref, logf_ref, *rest,
                   nh, tk, packed):
    tm = x_ref.shape[0]
    ts = tri_ref.shape[0]
    if packed:
        ka_ref, vt_ref, ct_ref, carry_ref = rest

        @pl.when(pl.program_id(1) == 0)
        def _():
            carry_ref[...] = jnp.zeros_like(carry_ref)

    def project(r):
        rows = pl.ds(r * ts, ts)
        xn = _rmsnorm(x_ref[rows, :], g_ref[...]).astype(BF16)
        k = _dot(xn, wk_ref[...])
        v = _dot(xn, wv_ref[...])
        z = _dot(xn, wf_ref[...]) + bf_ref[...]
        return k, v, jnp.minimum(z, 0.0) - jnp.log1p(jnp.exp(-jnp.abs(z)))

    def emit(r, k, v, logf):
        logf_ref[pl.ds(r * ts, ts), :] = logf
        for src, dst in ((k, k_ref), (v, v_ref)):
            for hd in range(nh):
                dst[pl.ds(r * ts * nh + hd, ts, stride=nh), :] = _head_lanes(src, hd)[:, :HALF_LANES]
        if packed:
            c = _tri_cumsum(tri_ref[...], jnp.where(_lane_iota(ts) < nh, logf, 0.0)) + carry_ref[...]
            carry_ref[...] = c[ts - 1:ts, :]
            ct_ref[:, pl.ds(r * ts, ts)] = c.T[:nh, :]
            _pack_keys(k, c, ka_ref.at[:, pl.ds(r * ts, ts), :], nh)
            _pack_values(v, vt_ref.at[:, pl.ds(r * (ts // tk), ts // tk)], nh, tk)

    pending = project(0)
    for r in range(tm // ts):
        done = pending
        if r + 1 < tm // ts:
            pending = project(r + 1)
        emit(r, *done)


def _fox_kv(h, g, w_kvf, b_f, width, nh, batch, seq_len, tk=None):
    n, d = h.shape
    packed = tk is not None
    tm = _row_tile(seq_len if packed else n, 512)
    grid = (batch, seq_len // tm) if packed else (n // tm, 1)
    nblk = grid[1]
    assert width == nh * HALF_LANES and N_PIECES * nh <= AUG_QPIECE0 - AUG_PIECE0
    assert nh & (nh - 1) == 0 and AUG_PIECE0 % nh == 0
    wf = jnp.zeros((d, LANES), BF16).at[:, :nh].set(w_kvf[:, 2 * width:].astype(BF16))
    bf = jnp.zeros((1, LANES), F32).at[0, :nh].set(b_f)
    ts = min(tm, 256)
    assert tm % ts == 0
    tri = (jnp.arange(ts)[:, None] >= jnp.arange(ts)[None, :]).astype(BF16)
    row = lambda r, w: pl.BlockSpec((r, w), lambda i, j: (i * nblk + j, 0))
    out_shape = [jax.ShapeDtypeStruct((n * nh, HALF_LANES), F32), jax.ShapeDtypeStruct((n * nh, HALF_LANES), F32),
                 jax.ShapeDtypeStruct((n, LANES), F32)]
    out_specs = [row(tm * nh, HALF_LANES), row(tm * nh, HALF_LANES), row(tm, LANES)]
    scratch = []
    if packed:
        assert ts % tk == 0
        out_shape += [jax.ShapeDtypeStruct((batch, nh, seq_len, LANES), BF16),
                      jax.ShapeDtypeStruct((batch, nh, seq_len // tk, VT_ROWS, tk), BF16),
                      jax.ShapeDtypeStruct((batch, nh, seq_len), F32)]
        out_specs += [pl.BlockSpec((None, nh, tm, LANES), lambda i, j: (i, 0, j, 0)),
                      pl.BlockSpec((None, nh, tm // tk, VT_ROWS, tk), lambda i, j: (i, 0, j, 0, 0)),
                      pl.BlockSpec((None, nh, tm), lambda i, j: (i, 0, j))]
        scratch = [pltpu.VMEM((1, LANES), F32)]
    return pl.pallas_call(
        functools.partial(_fox_kv_kernel, nh=nh, tk=tk, packed=packed),
        out_shape=tuple(out_shape),
        grid=grid,
        in_specs=[row(tm, d), _resident((1, d)), _resident((d, width)), _resident((d, width)),
                  _resident((d, LANES)), _resident((1, LANES)), _resident((ts, ts))],
        out_specs=tuple(out_specs),
        scratch_shapes=scratch,
        compiler_params=_cparams(("parallel", "arbitrary")),
        name="fox_kv",
    )(h, g.reshape(1, d), w_kvf[:, :width].astype(BF16), w_kvf[:, width:2 * width].astype(BF16), wf, bf, tri)


def _cumsum_kernel(x_ref, tri_ref, o_ref, *, tc, nblk):
    tri = tri_ref[...]

    def body(i, carry):
        r0 = pl.multiple_of(i * tc, tc)
        c = _tri_cumsum(tri, x_ref[pl.ds(r0, tc), :]) + carry
        o_ref[pl.ds(r0, tc), :] = c
        return c[tc - 1:tc, :]

    lax.fori_loop(0, nblk, body, jnp.zeros((1, x_ref.shape[-1]), F32))


def _cumsum_rows(x):
    b, n, w = x.shape
    tc = 128 if n % 128 == 0 else 64
    assert n % tc == 0
    tri = (jnp.arange(tc)[:, None] >= jnp.arange(tc)[None, :]).astype(BF16)
    blk = pl.BlockSpec((None, n, w), lambda i: (i, 0, 0))
    return pl.pallas_call(
        functools.partial(_cumsum_kernel, tc=tc, nblk=n // tc),
        out_shape=jax.ShapeDtypeStruct((b, n, w), F32),
        grid=(b,),
        in_specs=[blk, _resident((tc, tc))],
        out_specs=blk,
        compiler_params=_cparams(("parallel",)),
        name="cumsum",
    )(x, tri)


def _kpack_kernel(k_ref, v_ref, c_ref, ka_ref, vt_ref, *, nh, tk):
    _pack_keys(k_ref[...], c_ref[...], ka_ref, nh)
    _pack_values(v_ref[...], vt_ref, nh, tk)


def _kpack(k, v, c, nh, tk):
    b, n, width = k.shape
    tm = max(tk, 512) if n % max(tk, 512) == 0 else tk
    assert n % tm == 0 and tm % tk == 0
    row = lambda w: pl.BlockSpec((None, tm, w), lambda i, j: (i, j, 0))
    return pl.pallas_call(
        functools.partial(_kpack_kernel, nh=nh, tk=tk),
        out_shape=(jax.ShapeDtypeStruct((b, nh, n, LANES), BF16),
                   jax.ShapeDtypeStruct((b, nh, n // tk, VT_ROWS, tk), BF16)),
        grid=(b, n // tm),
        in_specs=[row(width), row(width), row(LANES)],
        out_specs=(pl.BlockSpec((None, nh, tm, LANES), lambda i, j: (i, 0, j, 0)),
                   pl.BlockSpec((None, nh, tm // tk, VT_ROWS, tk), lambda i, j: (i, 0, j, 0, 0))),
        compiler_params=_cparams(("parallel", "parallel")),
        name="kpack",
    )(k, v, c)


def _fox_q_kernel(x_ref, g_ref, wt_ref, ct_ref, qa_ref, *, nh):
    tm = x_ref.shape[0]
    xn = _rmsnorm(x_ref[...], g_ref[...]).astype(BF16)
    qt = _dot_nt(wt_ref[...], xn) * (HALF_LANES ** -0.5 * LOG2E)
    hi, mid, lo = _split3(ct_ref[...] * LOG2E)
    sub = lax.broadcasted_iota(jnp.int32, (8, tm), 0)
    rowp = lax.broadcasted_iota(jnp.int32, (N_PIECES * nh, tm), 0)
    tail = jnp.zeros((LANES - AUG_QPIECE0 - 8, tm), F32)
    for hd in range(nh):
        onehot = jnp.where((rowp & (nh - 1)) == hd, 1.0, 0.0)
        own = [jnp.broadcast_to(a[hd:hd + 1, :], (8, tm)) for a in (hi, mid, lo)]
        pieces = jnp.where(sub == 0, own[0], jnp.where(sub == 1, own[1], jnp.where(sub == 2, own[2], 0.0)))
        tile = jnp.concatenate([qt[hd * HALF_LANES:(hd + 1) * HALF_LANES, :], onehot, pieces, tail], axis=0)
        qa_ref[hd] = tile.astype(BF16)


def _fox_q(h, g, w_q, cq_t, batch, seq_len, nh):
    n, d = h.shape
    tm = _row_tile(seq_len, 1024)
    nblk = seq_len // tm
    return pl.pallas_call(
        functools.partial(_fox_q_kernel, nh=nh),
        out_shape=jax.ShapeDtypeStruct((batch, nh, LANES, seq_len), BF16),
        grid=(batch, nblk),
        in_specs=[
            pl.BlockSpec((tm, d), lambda b, j: (b * nblk + j, 0)),
            _resident((1, d)),
            _resident((nh * HALF_LANES, d)),
            pl.BlockSpec((None, nh, tm), lambda b, j: (b, 0, j)),
        ],
        out_specs=pl.BlockSpec((None, nh, LANES, tm), lambda b, j: (b, 0, 0, j)),
        compiler_params=_cparams(("parallel", "parallel")),
        name="fox_q",
    )(h, g.reshape(1, d), w_q.T.astype(BF16), cq_t)


def _fox_attn_kernel(q_ref, k_ref, v_ref, o_ref, s0_ref, s1_ref, acc_ref, m_ref, *, hg, tq, tk, q_off,
                     single_block):
    i = pl.program_id(2)
    q0 = q_off + i * tq
    n_full, n_mask = (0, 1) if single_block else (q0 // tk, tq // tk)
    acc_ref[...] = jnp.zeros_like(acc_ref)
    m_ref[...] = jnp.full_like(m_ref, -jnp.inf)
    slots = (s0_ref, s1_ref)
    cw = min(tq, 256)

    def scores_head(hh, j, slot, c0=0, c1=tq):
        r0 = pl.multiple_of(j * tk, tk)
        slots[slot][hh, :, c0:c1] = _dot(k_ref[hh, pl.ds(r0, tk), :], q_ref[hh, :, c0:c1])

    def absorb_head(hh, j, slot, c0=0, c1=tq, diag=None):
        w = c1 - c0
        s = slots[slot][hh, :, c0:c1]
        if diag is not None:
            kpos = lax.broadcasted_iota(jnp.int32, (tk, w), 0)
            qpos = lax.broadcasted_iota(jnp.int32, (tk, w), 1)
            s = jnp.where(qpos + diag >= kpos, s, -jnp.inf)
        m_prev = m_ref[hh, :, c0:c1]
        m_new = jnp.maximum(m_prev, jnp.max(s, axis=0, keepdims=True))
        alpha = jnp.exp2(m_prev - m_new)
        p = jnp.exp2(s - m_new).astype(BF16)
        acc_ref[hh, :, c0:c1] = alpha * acc_ref[hh, :, c0:c1] + _dot(v_ref[hh, j], p)
        m_ref[hh, :, c0:c1] = m_new

    def stage(j, slot):
        for hh in range(hg):
            for c0 in range(0, tq, cw):
                scores_head(hh, j + 1, 1 - slot, c0, c0 + cw)
                absorb_head(hh, j, slot, c0, c0 + cw)

    for hh in range(hg):
        scores_head(hh, 0, 0)

    def quad(jj, carry):
        for u in range(4):
            stage(4 * jj + u, u % 2)
        return carry

    lax.fori_loop(0, n_full // 4, quad, 0)

    @pl.when(n_full % 4 == 2)
    def _():
        stage(n_full - 2, 0)
        stage(n_full - 1, 1)
    for t in range(n_mask):
        lo = 0 if single_block else t * tk
        hi = min(lo + tk, tq)
        for hh in range(hg):
            if t + 1 < n_mask:
                scores_head(hh, n_full + t + 1, (t + 1) % 2, hi)
            absorb_head(hh, n_full + t, t % 2, lo, hi, diag=q_off if single_block else 0)
            if hi < tq:
                absorb_head(hh, n_full + t, t % 2, hi, tq)
    for g in range(hg // 2):
        halves = []
        for hh in (2 * g, 2 * g + 1):
            a = acc_ref[hh]
            halves.append(a[:HALF_LANES, :] / a[HALF_LANES:HALF_LANES + 1, :])
        o_ref[:, g * LANES:(g + 1) * LANES] = jnp.concatenate(halves, axis=0).T.astype(BF16)


def _fox_attn(qa, ka, vt, q_off):
    b, nh, _, seq_len = qa.shape
    n_keys = ka.shape[2]
    tk = vt.shape[-1]
    hg = 8
    tq = min(seq_len, 512)
    assert seq_len % tq == 0 and tq % LANES == 0 and nh % hg == 0 and q_off + seq_len <= n_keys
    nq = seq_len // tq
    single_block = n_keys == tk
    assert (single_block and nq == 1) or (tq % (2 * tk) == 0 and q_off % (2 * tk) == 0)
    return pl.pallas_call(
        functools.partial(_fox_attn_kernel, hg=hg, tq=tq, tk=tk, q_off=q_off, single_block=single_block),
        out_shape=jax.ShapeDtypeStruct((b * seq_len, nh * HALF_LANES), BF16),
        grid=(b, nh // hg, nq),
        in_specs=[
            pl.BlockSpec((None, hg, LANES, tq), lambda bi, g, i: (bi, g, 0, i)),
            pl.BlockSpec((None, hg, n_keys, LANES), lambda bi, g, i: (bi, g, 0, 0)),
            pl.BlockSpec((None, hg, n_keys // tk, VT_ROWS, tk), lambda bi, g, i: (bi, g, 0, 0, 0)),
        ],
        out_specs=pl.BlockSpec((tq, hg * HALF_LANES), lambda bi, g, i: (bi * nq + i, g)),
        scratch_shapes=[pltpu.VMEM((hg, tk, tq), F32), pltpu.VMEM((hg, tk, tq), F32),
                        pltpu.VMEM((hg, VT_ROWS, tq), F32), pltpu.VMEM((hg, 1, tq), F32)],
        compiler_params=_cparams(("parallel", "parallel", "arbitrary")),
        name="fox_attn",
    )(qa, ka, vt)


def _trunk(x, pos0, chunk, s0, past, p):
    batch, seq_len, d = x.shape
    nh_ret, dk, dv = s0.shape[2:]
    rv = nh_ret * dv
    nh_fox = p['fox_b_f'].shape[0]
    width = p['fox_w_q'].shape[-1]
    pos = pos0 + jnp.arange(seq_len)
    h = x.reshape(batch * seq_len, d)

    h = _ffn(h, p['ffn1_g'][0], p['ffn1_w_in'][0], p['ffn1_w_out'][0])
    q, k, v, sg = _ret_proj(h, p['mix_g'][0], p['ret_w_in'][0], pos, seq_len, nh_ret, rv)
    o, s_fin = _retention(q, k, v, sg, s0[0], p['ret_gn_g'][0], batch, seq_len, chunk)
    h = _proj_res(o, p['ret_w_out'][0], h)
    h = _ffn(h, p['ffn2_g'][0], p['ffn2_w_in'][0], p['ffn2_w_out'][0])

    head_dim = width // nh_fox
    pad_q = -seq_len % LANES
    if past is None:
        assert pad_q == 0
        q_off = 0
        k_new, v_new, logf_pad, ka, vt, cq_t = _fox_kv(h, p['kv_g'], p['fox_w_kvf'], p['fox_b_f'], width, nh_fox,
                                                       batch, seq_len, tk=256)
    else:
        k_new, v_new, logf_pad = _fox_kv(h, p['kv_g'], p['fox_w_kvf'], p['fox_b_f'], width, nh_fox, batch, seq_len)
    k_new = k_new.reshape(batch, seq_len, nh_fox, head_dim)
    v_new = v_new.reshape(batch, seq_len, nh_fox, head_dim)
    logf_new = logf_pad.reshape(batch, seq_len, LANES)[:, :, :nh_fox]
    if past is not None:
        past_len = past[0].shape[1]
        k_all, v_all = (jnp.concatenate([old.reshape(batch, past_len, width), new.reshape(batch, seq_len, width)],
                                        axis=1) for old, new in ((past[0], k_new), (past[1], v_new)))
        logf_all = jnp.concatenate([past[2], logf_new], axis=1)
        n_keys = past_len + seq_len
        q_off = past_len
        pad_k = -n_keys % LANES
        assert pad_q <= pad_k
        if pad_k:
            k_all, v_all, logf_all = (jnp.pad(t, ((0, 0), (0, pad_k), (0, 0))) for t in (k_all, v_all, logf_all))
        n_pad = n_keys + pad_k
        tk = 256 if n_pad % 256 == 0 else n_pad
        cumf = _cumsum_rows(jnp.pad(logf_all, ((0, 0), (0, 0), (0, LANES - nh_fox))))
        ka, vt = _kpack(k_all, v_all, cumf, nh_fox, tk)
        cq_t = jnp.swapaxes(cumf[:, q_off:q_off + seq_len, :nh_fox], 1, 2)

    h = _ffn(h, p['ffn1_g'][1], p['ffn1_w_in'][1], p['ffn1_w_out'][1])
    qa = _fox_q(h, p['mix_g'][1], p['fox_w_q'][0], cq_t, batch, seq_len, nh_fox)
    if pad_q:
        qa = jnp.pad(qa, ((0, 0), (0, 0), (0, 0), (0, pad_q)))
    o = _fox_attn(qa, ka, vt, q_off)
    if pad_q:
        o = o.reshape(batch, seq_len + pad_q, width)[:, :seq_len].reshape(batch * seq_len, width)
    h = _proj_res(o, p['fox_w_out'][0], h)
    y = _ffn(h, p['ffn2_g'][1], p['ffn2_w_in'][1], p['ffn2_w_out'][1], final_g=p['final_g'])

    return (y.reshape(batch, seq_len, d), s_fin[None], k_new, v_new, logf_new)


def kernel(x_prompt, x_sample, state_ret, cache_k, cache_v, cache_logf, ffn1_g, ffn1_w_in, ffn1_w_out, mix_g,
           ffn2_g, ffn2_w_in, ffn2_w_out, ret_w_in, ret_gn_g, ret_w_out, kv_g, fox_w_kvf, fox_b_f, fox_w_q,
           fox_w_out, final_g):
    p = {'ffn1_g': ffn1_g, 'ffn1_w_in': ffn1_w_in, 'ffn1_w_out': ffn1_w_out, 'mix_g': mix_g,
         'ffn2_g': ffn2_g, 'ffn2_w_in': ffn2_w_in, 'ffn2_w_out': ffn2_w_out,
         'ret_w_in': ret_w_in, 'ret_gn_g': ret_gn_g, 'ret_w_out': ret_w_out,
         'kv_g': kv_g, 'fox_w_kvf': fox_w_kvf, 'fox_b_f': fox_b_f,
         'fox_w_q': fox_w_q, 'fox_w_out': fox_w_out, 'final_g': final_g}
    assert state_ret.shape[0] == 1 and fox_w_q.shape[0] == 1
    for name in ('ffn1_w_in', 'ffn1_w_out', 'ffn2_w_in', 'ffn2_w_out', 'ret_w_in', 'ret_w_out', 'fox_w_kvf',
                 'fox_w_q', 'fox_w_out'):
        p[name] = _to_bf16(p[name])
    s0_prompt = jnp.zeros((1, x_prompt.shape[0]) + state_ret.shape[2:], F32)
    y_p, s_p, k_p, v_p, f_p = _trunk(x_prompt, 0, RET_CHUNK, s0_prompt, None, p)
    y_s, s_s, k_s, v_s, f_s = _trunk(x_sample, cache_k.shape[1], x_sample.shape[1], state_ret,
                                     (cache_k, cache_v, cache_logf), p)
    return (y_p, y_s, s_p, k_p, v_p, f_p, s_s, k_s, v_s, f_s)
```

```python
import functools

import jax
import jax.numpy as jnp
from jax import lax
from jax.experimental import pallas as pl
from jax.experimental.pallas import tpu as pltpu

F32 = jnp.float32
BF16 = jnp.bfloat16

EPS = 1e-6
ROPE_BASE = 10000.0
RET_CHUNK = 64
LANES = 128
HALF_LANES = LANES // 2
V7X_VMEM_LIMIT = 56 * 1024 * 1024

AUG_PIECE0 = 64
AUG_QPIECE0 = 112
N_PIECES = 3
VT_ROWS = 80
LOG2E = 1.4426950408889634


def _cparams(sem, flags=None):
    return pltpu.CompilerParams(dimension_semantics=sem, vmem_limit_bytes=V7X_VMEM_LIMIT, flags=flags)


def _resident(shape):
    nd = len(shape)
    return pl.BlockSpec(shape, lambda *_: (0,) * nd, pipeline_mode=pl.Buffered(1))


def _rmsnorm(x, g):
    ms = jnp.mean(x * x, axis=-1, keepdims=True)
    return x * lax.rsqrt(ms + EPS) * g


def _silu(x):
    return x * jax.nn.sigmoid(x)


def _dot(a, b):
    return jnp.dot(a, b, preferred_element_type=F32)


def _dot_nt(a, b):
    return lax.dot_general(a, b, (((1,), (1,)), ((), ())), preferred_element_type=F32)


def _dot_tn(a, b):
    return lax.dot_general(a, b, (((0,), (0,)), ((), ())), preferred_element_type=F32)


def _split3(c):
    hi = c.astype(BF16).astype(F32)
    r = c - hi
    mid = r.astype(BF16).astype(F32)
    lo = (r - mid).astype(BF16).astype(F32)
    return hi, mid, lo


def _row_tile(n, want):
    t = min(n, want)
    assert n % t == 0 and t % 8 == 0, (n, t)
    return t


def _ffn_kernel(x_ref, g_ref, win_ref, wout_ref, fg_ref, o_ref, act_ref, *, d_ff, ck, final_norm):
    tm = x_ref.shape[0]
    ts = min(tm, 512)
    norm = lambda r: _rmsnorm(x_ref[pl.ds(r * ts, ts), :], g_ref[...]).astype(BF16)
    xn = norm(0)
    for r in range(tm // ts):
        rows = pl.ds(r * ts, ts)
        xn_next = None
        for c in range(d_ff // ck):
            gate = _dot(xn, win_ref[:, c * ck:(c + 1) * ck])
            up = _dot(xn, win_ref[:, d_ff + c * ck:d_ff + (c + 1) * ck])
            act_ref[rows, c * ck:(c + 1) * ck] = (_silu(gate) * up).astype(BF16)
            if c == 0 and r + 1 < tm // ts:
                xn_next = norm(r + 1)
        y = x_ref[rows, :] + 0.5 * _dot(act_ref[rows, :], wout_ref[...])
        if final_norm:
            y = _rmsnorm(y, fg_ref[...])
        o_ref[rows, :] = y
        xn = xn_next


def _ffn(h, g, w_in, w_out, final_g=None):
    n, d = h.shape
    d_ff = w_out.shape[0]
    tm = _row_tile(n, 1024)
    ck = 256
    assert d_ff % ck == 0
    fg = jnp.ones((d,), F32) if final_g is None else final_g
    return pl.pallas_call(
        functools.partial(_ffn_kernel, d_ff=d_ff, ck=ck, final_norm=final_g is not None),
        out_shape=jax.ShapeDtypeStruct((n, d), F32),
        grid=(n // tm,),
        in_specs=[
            pl.BlockSpec((tm, d), lambda i: (i, 0)),
            _resident((1, d)),
            _resident((d, 2 * d_ff)),
            _resident((d_ff, d)),
            _resident((1, d)),
        ],
        out_specs=pl.BlockSpec((tm, d), lambda i: (i, 0)),
        scratch_shapes=[pltpu.VMEM((tm, d_ff), BF16)],
        compiler_params=_cparams(("parallel",)),
        name="ffn",
    )(h, g.reshape(1, d), w_in.astype(BF16), w_out.astype(BF16), fg.reshape(1, d))


def _proj_res_kernel(a_ref, w_ref, r_ref, o_ref):
    o_ref[...] = r_ref[...] + _dot(a_ref[...], w_ref[...])


def _proj_res(a, w, res):
    n, k = a.shape
    d = w.shape[1]
    tm = _row_tile(n, 1024)
    return pl.pallas_call(
        _proj_res_kernel,
        out_shape=jax.ShapeDtypeStruct((n, d), F32),
        grid=(n // tm,),
        in_specs=[
            pl.BlockSpec((tm, k), lambda i: (i, 0)),
            _resident((k, d)),
            pl.BlockSpec((tm, d), lambda i: (i, 0)),
        ],
        out_specs=pl.BlockSpec((tm, d), lambda i: (i, 0)),
        compiler_params=_cparams(("parallel",)),
        name="proj_res",
    )(a, w.astype(BF16), res)


def _rope_tables(pos, half):
    inv = ROPE_BASE ** (-jnp.arange(half, dtype=F32) / half)
    ang = pos.astype(F32)[:, None] * inv[None, :]
    return jnp.cos(ang), jnp.sin(ang)


def _ret_proj_kernel(x_ref, g_ref, w_ref, cos_ref, sin_ref, q_ref, k_ref, v_ref, sg_ref, *, d, nh, rv):
    dk = d // nh
    half = dk // 2
    tm = x_ref.shape[0]
    ts = min(tm, 256)
    norm = lambda r: _rmsnorm(x_ref[pl.ds(r * ts, ts), :], g_ref[...]).astype(BF16)
    xn = norm(0)
    for r in range(tm // ts):
        rows = pl.ds(r * ts, ts)
        cos = cos_ref[rows, :]
        sin = sin_ref[rows, :]
        sg_ref[rows, :] = _silu(_dot(xn, w_ref[:, 2 * d + rv:2 * d + 2 * rv])).astype(BF16)
        xn_next = norm(r + 1) if r + 1 < tm // ts else None
        for off, out_ref, scale in ((0, q_ref, dk ** -0.5), (d, k_ref, 1.0)):
            t = _dot(xn, w_ref[:, off:off + d])
            for hd in range(nh):
                x1 = t[:, hd * dk:hd * dk + half]
                x2 = t[:, hd * dk + half:(hd + 1) * dk]
                out_ref[rows, hd * dk:hd * dk + half] = ((x1 * cos - x2 * sin) * scale).astype(BF16)
                out_ref[rows, hd * dk + half:(hd + 1) * dk] = ((x1 * sin + x2 * cos) * scale).astype(BF16)
        v_ref[rows, :] = _dot(xn, w_ref[:, 2 * d:2 * d + rv]).astype(BF16)
        xn = xn_next


def _ret_proj(h, g, w_in, pos, seq_len, nh, rv):
    n, d = h.shape
    half = d // nh // 2
    tm = _row_tile(n, 512)
    cos, sin = _rope_tables(pos, half)
    if seq_len % tm == 0:
        period = seq_len // tm
    else:
        assert tm % seq_len == 0
        cos, sin = (jnp.tile(t, (tm // seq_len, 1)) for t in (cos, sin))
        period = 1
    tab = pl.BlockSpec((tm, half), lambda i: (i % period, 0))
    row = lambda w: pl.BlockSpec((tm, w), lambda i: (i, 0))
    return pl.pallas_call(
        functools.partial(_ret_proj_kernel, d=d, nh=nh, rv=rv),
        out_shape=(jax.ShapeDtypeStruct((n, d), BF16), jax.ShapeDtypeStruct((n, d), BF16),
                   jax.ShapeDtypeStruct((n, rv), BF16), jax.ShapeDtypeStruct((n, rv), BF16)),
        grid=(n // tm,),
        in_specs=[row(d), _resident((1, d)), _resident((d, 2 * d + 2 * rv)), tab, tab],
        out_specs=(row(d), row(d), row(rv), row(rv)),
        compiler_params=_cparams(("parallel",)),
        name="ret_proj",
    )(h, g.reshape(1, d), w_in.astype(BF16), cos, sin)


def _ret_decay_tables(nh, t, chunk):
    log_gamma = jnp.log1p(-jnp.exp2(-5.0 - jnp.arange(nh, dtype=F32)))
    pos = jnp.arange(t, dtype=F32)
    dist = jnp.abs(pos[:, None] - pos[None, :])
    cid = jnp.arange(t) // chunk
    visible = (cid[None, :] <= cid[:, None]).astype(F32)
    dmask = jnp.exp(log_gamma[:, None, None] * dist) * visible[None]
    qdec = jnp.exp(log_gamma[:, None] * (pos[None, :] + 1.0))[:, :, None]
    kdec = jnp.exp(log_gamma[:, None] * (t - 1.0 - pos[None, :]))[:, :, None]
    sdec = jnp.exp(log_gamma * t)
    return dmask, qdec, kdec, sdec


def _retention_kernel(sdec_ref, q_ref, k_ref, v_ref, sg_ref, dmask_ref, qdec_ref, kdec_ref, gn_ref, s0_ref,
                      o_ref, sout_ref, s_ref, *, nh, dk, dv):
    j = pl.program_id(1)

    @pl.when(j == 0)
    def _():
        s_ref[...] = s0_ref[...]

    for hd in range(nh):
        qh = q_ref[:, hd * dk:(hd + 1) * dk]
        kh = k_ref[:, hd * dk:(hd + 1) * dk]
        vh = v_ref[:, hd * dv:(hd + 1) * dv]
        state = s_ref[hd]
        scores = _dot_nt(qh, kh) * dmask_ref[hd]
        o = _dot(scores.astype(BF16), vh) + _dot(qh, state.astype(BF16)) * qdec_ref[hd]
        kd = (kh.astype(F32) * kdec_ref[hd]).astype(BF16)
        s_ref[hd] = sdec_ref[hd] * state + _dot_tn(kd, vh)
        mu = jnp.mean(o, axis=-1, keepdims=True)
        oc = o - mu
        var = jnp.mean(oc * oc, axis=-1, keepdims=True)
        on = oc * lax.rsqrt(var + EPS) * gn_ref[:, hd * dv:(hd + 1) * dv]
        o_ref[:, hd * dv:(hd + 1) * dv] = (sg_ref[:, hd * dv:(hd + 1) * dv].astype(F32) * on).astype(BF16)

    @pl.when(j == pl.num_programs(1) - 1)
    def _():
        sout_ref[...] = s_ref[...]


def _retention(q, k, v, sg, s0, gn_g, batch, seq_len, chunk):
    nh, dk, dv = s0.shape[1:]
    rv = nh * dv
    t = _row_tile(seq_len, max(chunk, 256))
    assert t % chunk == 0
    nblk = seq_len // t
    dmask, qdec, kdec, sdec = _ret_decay_tables(nh, t, chunk)
    row = lambda w: pl.BlockSpec((t, w), lambda b, j: (b * nblk + j, 0))
    st = pl.BlockSpec((None, nh, dk, dv), lambda b, j: (b, 0, 0, 0))
    return pl.pallas_call(
        functools.partial(_retention_kernel, nh=nh, dk=dk, dv=dv),
        out_shape=(jax.ShapeDtypeStruct((batch * seq_len, rv), BF16),
                   jax.ShapeDtypeStruct((batch, nh, dk, dv), F32)),
        grid=(batch, nblk),
        in_specs=[
            pl.BlockSpec(memory_space=pltpu.SMEM),
            row(nh * dk), row(nh * dk), row(rv), row(rv),
            _resident((nh, t, t)), _resident((nh, t, 1)), _resident((nh, t, 1)), _resident((1, rv)),
            st,
        ],
        out_specs=(row(rv), st),
        scratch_shapes=[pltpu.VMEM((nh, dk, dv), F32)],
        compiler_params=_cparams(("parallel", "arbitrary")),
        name="retention",
    )(sdec, q, k, v, sg, dmask, qdec, kdec, gn_g.reshape(1, rv), s0)


def _lane_iota(rows):
    return lax.broadcasted_iota(jnp.int32, (rows, LANES), 1)


def _head_lanes(x, hd):
    t = x[:, (hd // 2) * LANES:(hd // 2 + 1) * LANES]
    return pltpu.roll(t, HALF_LANES, axis=1) if hd % 2 else t


def _tri_cumsum(tri, x):
    hi, mid, lo = _split3(x)
    return _dot(tri, hi.astype(BF16)) + _dot(tri, mid.astype(BF16)) + _dot(tri, lo.astype(BF16))


def _pack_keys(k, c, ka_ref, nh):
    rows = k.shape[0]
    lane = _lane_iota(rows)
    hi, mid, lo = _split3(c * LOG2E)
    extra = jnp.where(lane < AUG_QPIECE0 + N_PIECES, 1.0, 0.0)
    for p, piece in reversed(list(enumerate((hi, mid, lo)))):
        lo_lane = AUG_PIECE0 + p * nh
        extra = jnp.where(lane < lo_lane + nh, -pltpu.roll(piece, lo_lane, axis=1), extra)
    for hd in range(nh):
        ka_ref[hd] = jnp.where(lane < HALF_LANES, _head_lanes(k, hd), extra).astype(BF16)


def _pack_values(v, vt_ref, nh, tk):
    rows = v.shape[0]
    vt = v.T
    sub = lax.broadcasted_iota(jnp.int32, (VT_ROWS - HALF_LANES, rows), 0)
    ones_row = jnp.where(sub == 0, 1.0, 0.0)
    for hd in range(nh):
        t = jnp.concatenate([vt[hd * HALF_LANES:(hd + 1) * HALF_LANES, :], ones_row], axis=0).astype(BF16)
        for s in range(rows // tk):
            vt_ref[hd, s] = t[:, s * tk:(s + 1) * tk]


def _fox_kv_kernel(x_ref, g_ref, wk_ref, wv_ref, wf_ref, bf_ref, tri_ref, k_ref, v_ref, logf_ref, *rest,
                   nh, tk, packed):
    tm = x_ref.shape[0]
    ts = tri_ref.shape[0]
    if packed:
        ka_ref, vt_ref, ct_ref, carry_ref = rest

        @pl.when(pl.program_id(1) == 0)
        def _():
            carry_ref[...] = jnp.zeros_like(carry_ref)

    def project(r):
        rows = pl.ds(r * ts, ts)
        xn = _rmsnorm(x_ref[rows, :], g_ref[...]).astype(BF16)
        k = _dot(xn, wk_ref[...])
        v = _dot(xn, wv_ref[...])
        z = _dot(xn, wf_ref[...]) + bf_ref[...]
        return k, v, jnp.minimum(z, 0.0) - jnp.log1p(jnp.exp(-jnp.abs(z)))

    def emit(r, k, v, logf):
        logf_ref[pl.ds(r * ts, ts), :] = logf
        for src, dst in ((k, k_ref), (v, v_ref)):
            for hd in range(nh):
                dst[pl.ds(r * ts * nh + hd, ts, stride=nh), :] = _head_lanes(src, hd)[:, :HALF_LANES]
        if packed:
            c = _tri_cumsum(tri_ref[...], jnp.where(_lane_iota(ts) < nh, logf, 0.0)) + carry_ref[...]
            carry_ref[...] = c[ts - 1:ts, :]
            ct_ref[:, pl.ds(r * ts, ts)] = c.T[:nh, :]
            _pack_keys(k, c, ka_ref.at[:, pl.ds(r * ts, ts), :], nh)
            _pack_values(v, vt_ref.at[:, pl.ds(r * (ts // tk), ts // tk)], nh, tk)

    pending = project(0)
    for r in range(tm // ts):
        done = pending
        if r + 1 < tm // ts:
            pending = project(r + 1)
        emit(r, *done)


def _fox_kv(h, g, w_kvf, b_f, width, nh, batch, seq_len, tk=None):
    n, d = h.shape
    packed = tk is not None
    tm = _row_tile(seq_len if packed else n, 512)
    grid = (batch, seq_len // tm) if packed else (n // tm, 1)
    nblk = grid[1]
    assert width == nh * HALF_LANES and N_PIECES * nh <= AUG_QPIECE0 - AUG_PIECE0
    assert nh & (nh - 1) == 0 and AUG_PIECE0 % nh == 0
    wf = jnp.zeros((d, LANES), BF16).at[:, :nh].set(w_kvf[:, 2 * width:].astype(BF16))
    bf = jnp.zeros((1, LANES), F32).at[0, :nh].set(b_f)
    ts = min(tm, 256)
    assert tm % ts == 0
    tri = (jnp.arange(ts)[:, None] >= jnp.arange(ts)[None, :]).astype(BF16)
    row = lambda r, w: pl.BlockSpec((r, w), lambda i, j: (i * nblk + j, 0))
    out_shape = [jax.ShapeDtypeStruct((n * nh, HALF_LANES), F32), jax.ShapeDtypeStruct((n * nh, HALF_LANES), F32),
                 jax.ShapeDtypeStruct((n, LANES), F32)]
    out_specs = [row(tm * nh, HALF_LANES), row(tm * nh, HALF_LANES), row(tm, LANES)]
    scratch = []
    if packed:
        assert ts % tk == 0
        out_shape += [jax.ShapeDtypeStruct((batch, nh, seq_len, LANES), BF16),
                      jax.ShapeDtypeStruct((batch, nh, seq_len // tk, VT_ROWS, tk), BF16),
                      jax.ShapeDtypeStruct((batch, nh, seq_len), F32)]
        out_specs += [pl.BlockSpec((None, nh, tm, LANES), lambda i, j: (i, 0, j, 0)),
                      pl.BlockSpec((None, nh, tm // tk, VT_ROWS, tk), lambda i, j: (i, 0, j, 0, 0)),
                      pl.BlockSpec((None, nh, tm), lambda i, j: (i, 0, j))]
        scratch = [pltpu.VMEM((1, LANES), F32)]
    return pl.pallas_call(
        functools.partial(_fox_kv_kernel, nh=nh, tk=tk, packed=packed),
        out_shape=tuple(out_shape),
        grid=grid,
        in_specs=[row(tm, d), _resident((1, d)), _resident((d, width)), _resident((d, width)),
                  _resident((d, LANES)), _resident((1, LANES)), _resident((ts, ts))],
        out_specs=tuple(out_specs),
        scratch_shapes=scratch,
        compiler_params=_cparams(("parallel", "arbitrary")),
        name="fox_kv",
    )(h, g.reshape(1, d), w_kvf[:, :width].astype(BF16), w_kvf[:, width:2 * width].astype(BF16), wf, bf, tri)


def _cumsum_kernel(x_ref, tri_ref, o_ref, *, tc, nblk):
    tri = tri_ref[...]

    def body(i, carry):
        r0 = pl.multiple_of(i * tc, tc)
        c = _tri_cumsum(tri, x_ref[pl.ds(r0, tc), :]) + carry
        o_ref[pl.ds(r0, tc), :] = c
        return c[tc - 1:tc, :]

    lax.fori_loop(0, nblk, body, jnp.zeros((1, x_ref.shape[-1]), F32))


def _cumsum_rows(x):
    b, n, w = x.shape
    tc = 128 if n % 128 == 0 else 64
    assert n % tc == 0
    tri = (jnp.arange(tc)[:, None] >= jnp.arange(tc)[None, :]).astype(BF16)
    blk = pl.BlockSpec((None, n, w), lambda i: (i, 0, 0))
    return pl.pallas_call(
        functools.partial(_cumsum_kernel, tc=tc, nblk=n // tc),
        out_shape=jax.ShapeDtypeStruct((b, n, w), F32),
        grid=(b,),
        in_specs=[blk, _resident((tc, tc))],
        out_specs=blk,
        compiler_params=_cparams(("parallel",)),
        name="cumsum",
    )(x, tri)


def _fox_attn_cached_kernel(q_ref, k_ref, v_ref, c_ref, o_ref, ka_ref, vt_ref, *, nh, q_off):
    n = k_ref.shape[0]
    tq = q_ref.shape[-1]
    _pack_keys(k_ref[...], c_ref[...], ka_ref, nh)
    _pack_values(v_ref[...], vt_ref, nh, n)
    kpos = lax.broadcasted_iota(jnp.int32, (n, tq), 0)
    qpos = lax.broadcasted_iota(jnp.int32, (n, tq), 1)
    visible = qpos + q_off >= kpos
    scores = lambda hh: _dot(ka_ref[hh], q_ref[hh])
    halves = []
    pending = scores(0)
    for hh in range(nh):
        s = jnp.where(visible, pending, -jnp.inf)
        if hh + 1 < nh:
            pending = scores(hh + 1)
        p = jnp.exp2(s - jnp.max(s, axis=0, keepdims=True)).astype(BF16)
        a = _dot(vt_ref[hh, 0], p)
        halves.append(a[:HALF_LANES, :] / a[HALF_LANES:HALF_LANES + 1, :])
        if hh % 2:
            o_ref[:, (hh // 2) * LANES:(hh // 2 + 1) * LANES] = jnp.concatenate(halves, axis=0).T.astype(BF16)
            halves = []


def _fox_attn_cached(qa, k, v, c, q_off, nh):
    b, n, width = k.shape
    tq = qa.shape[-1]
    assert n % LANES == 0 and tq % LANES == 0 and q_off + tq <= n
    blk = lambda w: pl.BlockSpec((None, n, w), lambda i: (i, 0, 0))
    return pl.pallas_call(
        functools.partial(_fox_attn_cached_kernel, nh=nh, q_off=q_off),
        out_shape=jax.ShapeDtypeStruct((b * tq, width), BF16),
        grid=(b,),
        in_specs=[pl.BlockSpec((None, nh, LANES, tq), lambda i: (i, 0, 0, 0)), blk(width), blk(width), blk(LANES)],
        out_specs=pl.BlockSpec((tq, width), lambda i: (i, 0)),
        scratch_shapes=[pltpu.VMEM((nh, n, LANES), BF16), pltpu.VMEM((nh, 1, VT_ROWS, n), BF16)],
        compiler_params=_cparams(("parallel",)),
        name="fox_attn_cached",
    )(qa, k, v, c)


def _fox_q_kernel(x_ref, g_ref, wt_ref, ct_ref, qa_ref, *, nh):
    tm = x_ref.shape[0]
    xn = _rmsnorm(x_ref[...], g_ref[...]).astype(BF16)
    qt = _dot_nt(wt_ref[...], xn) * (HALF_LANES ** -0.5 * LOG2E)
    hi, mid, lo = _split3(ct_ref[...] * LOG2E)
    sub = lax.broadcasted_iota(jnp.int32, (8, tm), 0)
    rowp = lax.broadcasted_iota(jnp.int32, (N_PIECES * nh, tm), 0)
    tail = jnp.zeros((LANES - AUG_QPIECE0 - 8, tm), F32)
    for hd in range(nh):
        onehot = jnp.where((rowp & (nh - 1)) == hd, 1.0, 0.0)
        own = [jnp.broadcast_to(a[hd:hd + 1, :], (8, tm)) for a in (hi, mid, lo)]
        pieces = jnp.where(sub == 0, own[0], jnp.where(sub == 1, own[1], jnp.where(sub == 2, own[2], 0.0)))
        tile = jnp.concatenate([qt[hd * HALF_LANES:(hd + 1) * HALF_LANES, :], onehot, pieces, tail], axis=0)
        qa_ref[hd] = tile.astype(BF16)


def _fox_q(h, g, w_q, cq_t, batch, seq_len, nh):
    n, d = h.shape
    tm = _row_tile(seq_len, 1024)
    nblk = seq_len // tm
    return pl.pallas_call(
        functools.partial(_fox_q_kernel, nh=nh),
        out_shape=jax.ShapeDtypeStruct((batch, nh, LANES, seq_len), BF16),
        grid=(batch, nblk),
        in_specs=[
            pl.BlockSpec((tm, d), lambda b, j: (b * nblk + j, 0)),
            _resident((1, d)),
            _resident((nh * HALF_LANES, d)),
            pl.BlockSpec((None, nh, tm), lambda b, j: (b, 0, j)),
        ],
        out_specs=pl.BlockSpec((None, nh, LANES, tm), lambda b, j: (b, 0, 0, j)),
        compiler_params=_cparams(("parallel", "parallel")),
        name="fox_q",
    )(h, g.reshape(1, d), w_q.T.astype(BF16), cq_t)


def _fox_attn_kernel(q_ref, k_ref, v_ref, o_ref, s0_ref, s1_ref, acc_ref, m_ref, *, hg, tq, tk, q_off,
                     single_block):
    i = pl.program_id(2)
    q0 = q_off + i * tq
    n_full, n_mask = (0, 1) if single_block else (q0 // tk, tq // tk)
    acc_ref[...] = jnp.zeros_like(acc_ref)
    m_ref[...] = jnp.full_like(m_ref, -jnp.inf)
    slots = (s0_ref, s1_ref)
    cw = min(tq, 256)

    def scores_head(hh, j, slot, c0=0, c1=tq):
        r0 = pl.multiple_of(j * tk, tk)
        slots[slot][hh, :, c0:c1] = _dot(k_ref[hh, pl.ds(r0, tk), :], q_ref[hh, :, c0:c1])

    def absorb_head(hh, j, slot, c0=0, c1=tq, diag=None):
        w = c1 - c0
        s = slots[slot][hh, :, c0:c1]
        if diag is not None:
            kpos = lax.broadcasted_iota(jnp.int32, (tk, w), 0)
            qpos = lax.broadcasted_iota(jnp.int32, (tk, w), 1)
            s = jnp.where(qpos + diag >= kpos, s, -jnp.inf)
        m_prev = m_ref[hh, :, c0:c1]
        m_new = jnp.maximum(m_prev, jnp.max(s, axis=0, keepdims=True))
        alpha = jnp.exp2(m_prev - m_new)
        p = jnp.exp2(s - m_new).astype(BF16)
        acc_ref[hh, :, c0:c1] = alpha * acc_ref[hh, :, c0:c1] + _dot(v_ref[hh, j], p)
        m_ref[hh, :, c0:c1] = m_new

    def stage(j, slot):
        for hh in range(hg):
            for c0 in range(0, tq, cw):
                scores_head(hh, j + 1, 1 - slot, c0, c0 + cw)
                absorb_head(hh, j, slot, c0, c0 + cw)

    for hh in range(hg):
        scores_head(hh, 0, 0)

    def quad(jj, carry):
        for u in range(4):
            stage(4 * jj + u, u % 2)
        return carry

    lax.fori_loop(0, n_full // 4, quad, 0)

    @pl.when(n_full % 4 == 2)
    def _():
        stage(n_full - 2, 0)
        stage(n_full - 1, 1)
    for t in range(n_mask):
        lo = 0 if single_block else t * tk
        hi = min(lo + tk, tq)
        for hh in range(hg):
            if t + 1 < n_mask:
                scores_head(hh, n_full + t + 1, (t + 1) % 2, hi)
            absorb_head(hh, n_full + t, t % 2, lo, hi, diag=q_off if single_block else 0)
            if hi < tq:
                absorb_head(hh, n_full + t, t % 2, hi, tq)
    for g in range(hg // 2):
        halves = []
        for hh in (2 * g, 2 * g + 1):
            a = acc_ref[hh]
            halves.append(a[:HALF_LANES, :] / a[HALF_LANES:HALF_LANES + 1, :])
        o_ref[:, g * LANES:(g + 1) * LANES] = jnp.concatenate(halves, axis=0).T.astype(BF16)


def _fox_attn(qa, ka, vt, q_off):
    b, nh, _, seq_len = qa.shape
    n_keys = ka.shape[2]
    tk = vt.shape[-1]
    hg = 8
    tq = min(seq_len, 512)
    assert seq_len % tq == 0 and tq % LANES == 0 and nh % hg == 0 and q_off + seq_len <= n_keys
    nq = seq_len // tq
    single_block = n_keys == tk
    assert (single_block and nq == 1) or (tq % (2 * tk) == 0 and q_off % (2 * tk) == 0)
    return pl.pallas_call(
        functools.partial(_fox_attn_kernel, hg=hg, tq=tq, tk=tk, q_off=q_off, single_block=single_block),
        out_shape=jax.ShapeDtypeStruct((b * seq_len, nh * HALF_LANES), BF16),
        grid=(b, nh // hg, nq),
        in_specs=[
            pl.BlockSpec((None, hg, LANES, tq), lambda bi, g, i: (bi, g, 0, i)),
            pl.BlockSpec((None, hg, n_keys, LANES), lambda bi, g, i: (bi, g, 0, 0)),
            pl.BlockSpec((None, hg, n_keys // tk, VT_ROWS, tk), lambda bi, g, i: (bi, g, 0, 0, 0)),
        ],
        out_specs=pl.BlockSpec((tq, hg * HALF_LANES), lambda bi, g, i: (bi * nq + i, g)),
        scratch_shapes=[pltpu.VMEM((hg, tk, tq), F32), pltpu.VMEM((hg, tk, tq), F32),
                        pltpu.VMEM((hg, VT_ROWS, tq), F32), pltpu.VMEM((hg, 1, tq), F32)],
        compiler_params=_cparams(("parallel", "parallel", "arbitrary")),
        name="fox_attn",
    )(qa, ka, vt)


def _trunk(x, pos0, chunk, s0, past, p):
    batch, seq_len, d = x.shape
    nh_ret, dk, dv = s0.shape[2:]
    rv = nh_ret * dv
    nh_fox = p['fox_b_f'].shape[0]
    width = p['fox_w_q'].shape[-1]
    pos = pos0 + jnp.arange(seq_len)
    h = x.reshape(batch * seq_len, d)

    h = _ffn(h, p['ffn1_g'][0], p['ffn1_w_in'][0], p['ffn1_w_out'][0])
    q, k, v, sg = _ret_proj(h, p['mix_g'][0], p['ret_w_in'][0], pos, seq_len, nh_ret, rv)
    o, s_fin = _retention(q, k, v, sg, s0[0], p['ret_gn_g'][0], batch, seq_len, chunk)
    h = _proj_res(o, p['ret_w_out'][0], h)
    h = _ffn(h, p['ffn2_g'][0], p['ffn2_w_in'][0], p['ffn2_w_out'][0])

    head_dim = width // nh_fox
    pad_q = -seq_len % LANES
    if past is None:
        assert pad_q == 0
        q_off = 0
        k_new, v_new, logf_pad, ka, vt, cq_t = _fox_kv(h, p['kv_g'], p['fox_w_kvf'], p['fox_b_f'], width, nh_fox,
                                                       batch, seq_len, tk=256)
    else:
        k_new, v_new, logf_pad = _fox_kv(h, p['kv_g'], p['fox_w_kvf'], p['fox_b_f'], width, nh_fox, batch, seq_len)
    k_new = k_new.reshape(batch, seq_len, nh_fox, head_dim)
    v_new = v_new.reshape(batch, seq_len, nh_fox, head_dim)
    logf_new = logf_pad.reshape(batch, seq_len, LANES)[:, :, :nh_fox]
    if past is not None:
        past_len = past[0].shape[1]
        k_all, v_all = (jnp.concatenate([old.reshape(batch, past_len, width), new.reshape(batch, seq_len, width)],
                                        axis=1) for old, new in ((past[0], k_new), (past[1], v_new)))
        logf_all = jnp.concatenate([past[2], logf_new], axis=1)
        n_keys = past_len + seq_len
        q_off = past_len
        pad_k = -n_keys % LANES
        assert pad_q <= pad_k
        if pad_k:
            k_all, v_all, logf_all = (jnp.pad(t, ((0, 0), (0, pad_k), (0, 0))) for t in (k_all, v_all, logf_all))
        cumf = _cumsum_rows(jnp.pad(logf_all, ((0, 0), (0, 0), (0, LANES - nh_fox))))
        cq_t = jnp.swapaxes(cumf[:, q_off:q_off + seq_len, :nh_fox], 1, 2)

    h = _ffn(h, p['ffn1_g'][1], p['ffn1_w_in'][1], p['ffn1_w_out'][1])
    qa = _fox_q(h, p['mix_g'][1], p['fox_w_q'][0], cq_t, batch, seq_len, nh_fox)
    if pad_q:
        qa = jnp.pad(qa, ((0, 0), (0, 0), (0, 0), (0, pad_q)))
    if past is None:
        o = _fox_attn(qa, ka, vt, q_off)
    else:
        o = _fox_attn_cached(qa, k_all, v_all, cumf, q_off, nh_fox)
    if pad_q:
        o = o.reshape(batch, seq_len + pad_q, width)[:, :seq_len].reshape(batch * seq_len, width)
    h = _proj_res(o, p['fox_w_out'][0], h)
    y = _ffn(h, p['ffn2_g'][1], p['ffn2_w_in'][1], p['ffn2_w_out'][1], final_g=p['final_g'])

    return (y.reshape(batch, seq_len, d), s_fin[None], k_new, v_new, logf_new)


def kernel(x_prompt, x_sample, state_ret, cache_k, cache_v, cache_logf, ffn1_g, ffn1_w_in, ffn1_w_out, mix_g,
           ffn2_g, ffn2_w_in, ffn2_w_out, ret_w_in, ret_gn_g, ret_w_out, kv_g, fox_w_kvf, fox_b_f, fox_w_q,
           fox_w_out, final_g):
    p = {'ffn1_g': ffn1_g, 'ffn1_w_in': ffn1_w_in, 'ffn1_w_out': ffn1_w_out, 'mix_g': mix_g,
         'ffn2_g': ffn2_g, 'ffn2_w_in': ffn2_w_in, 'ffn2_w_out': ffn2_w_out,
         'ret_w_in': ret_w_in, 'ret_gn_g': ret_gn_g, 'ret_w_out': ret_w_out,
         'kv_g': kv_g, 'fox_w_kvf': fox_w_kvf, 'fox_b_f': fox_b_f,
         'fox_w_q': fox_w_q, 'fox_w_out': fox_w_out, 'final_g': final_g}
    assert state_ret.shape[0] == 1 and fox_w_q.shape[0] == 1
    s0_prompt = jnp.zeros((1, x_prompt.shape[0]) + state_ret.shape[2:], F32)
    y_p, s_p, k_p, v_p, f_p = _trunk(x_prompt, 0, RET_CHUNK, s0_prompt, None, p)
    y_s, s_s, k_s, v_s, f_s = _trunk(x_sample, cache_k.shape[1], x_sample.shape[1], state_ret,
                                     (cache_k, cache_v, cache_logf), p)
    return (y_p, y_s, s_p, k_p, v_p, f_p, s_s, k_s, v_s, f_s)
```

```python
import functools

import jax
import jax.numpy as jnp
from jax import lax
from jax.experimental import pallas as pl
from jax.experimental.pallas import tpu as pltpu

F32 = jnp.float32
BF16 = jnp.bfloat16

EPS = 1e-6
ROPE_BASE = 10000.0
RET_CHUNK = 64
LANES = 128
HALF_LANES = LANES // 2
V7X_VMEM_LIMIT = 56 * 1024 * 1024

AUG_PIECE0 = 64
AUG_QPIECE0 = 112
N_PIECES = 3
VT_ROWS = 80
LOG2E = 1.4426950408889634


def _cparams(sem, flags=None):
    return pltpu.CompilerParams(dimension_semantics=sem, vmem_limit_bytes=V7X_VMEM_LIMIT, flags=flags)


def _resident(shape):
    nd = len(shape)
    return pl.BlockSpec(shape, lambda *_: (0,) * nd, pipeline_mode=pl.Buffered(1))


def _rmsnorm(x, g):
    ms = jnp.mean(x * x, axis=-1, keepdims=True)
    return x * lax.rsqrt(ms + EPS) * g


def _silu(x):
    return x * jax.nn.sigmoid(x)


def _dot(a, b):
    return jnp.dot(a, b, preferred_element_type=F32)


def _dot_nt(a, b):
    return lax.dot_general(a, b, (((1,), (1,)), ((), ())), preferred_element_type=F32)


def _dot_tn(a, b):
    return lax.dot_general(a, b, (((0,), (0,)), ((), ())), preferred_element_type=F32)


def _split3(c):
    hi = c.astype(BF16).astype(F32)
    r = c - hi
    mid = r.astype(BF16).astype(F32)
    lo = (r - mid).astype(BF16).astype(F32)
    return hi, mid, lo


def _row_tile(n, want):
    t = min(n, want)
    assert n % t == 0 and t % 8 == 0, (n, t)
    return t


def _ffn_kernel(x_ref, g_ref, win_ref, wout_ref, fg_ref, o_ref, act_ref, *, d_ff, ck, final_norm):
    x = x_ref[...]
    xn = _rmsnorm(x, g_ref[...]).astype(BF16)
    for c in range(d_ff // ck):
        gate = _dot(xn, win_ref[:, c * ck:(c + 1) * ck])
        up = _dot(xn, win_ref[:, d_ff + c * ck:d_ff + (c + 1) * ck])
        act_ref[:, c * ck:(c + 1) * ck] = (_silu(gate) * up).astype(BF16)
    y = x + 0.5 * _dot(act_ref[...], wout_ref[...])
    if final_norm:
        y = _rmsnorm(y, fg_ref[...])
    o_ref[...] = y


def _ffn(h, g, w_in, w_out, final_g=None):
    n, d = h.shape
    d_ff = w_out.shape[0]
    tm = _row_tile(n, 1024)
    ck = 256
    assert d_ff % ck == 0
    fg = jnp.ones((d,), F32) if final_g is None else final_g
    return pl.pallas_call(
        functools.partial(_ffn_kernel, d_ff=d_ff, ck=ck, final_norm=final_g is not None),
        out_shape=jax.ShapeDtypeStruct((n, d), F32),
        grid=(n // tm,),
        in_specs=[
            pl.BlockSpec((tm, d), lambda i: (i, 0)),
            _resident((1, d)),
            _resident((d, 2 * d_ff)),
            _resident((d_ff, d)),
            _resident((1, d)),
        ],
        out_specs=pl.BlockSpec((tm, d), lambda i: (i, 0)),
        scratch_shapes=[pltpu.VMEM((tm, d_ff), BF16)],
        compiler_params=_cparams(("parallel",)),
        name="ffn",
    )(h, g.reshape(1, d), w_in.astype(BF16), w_out.astype(BF16), fg.reshape(1, d))


def _proj_res_kernel(a_ref, w_ref, r_ref, o_ref):
    o_ref[...] = r_ref[...] + _dot(a_ref[...], w_ref[...])


def _proj_res(a, w, res):
    n, k = a.shape
    d = w.shape[1]
    tm = _row_tile(n, 1024)
    return pl.pallas_call(
        _proj_res_kernel,
        out_shape=jax.ShapeDtypeStruct((n, d), F32),
        grid=(n // tm,),
        in_specs=[
            pl.BlockSpec((tm, k), lambda i: (i, 0)),
            _resident((k, d)),
            pl.BlockSpec((tm, d), lambda i: (i, 0)),
        ],
        out_specs=pl.BlockSpec((tm, d), lambda i: (i, 0)),
        compiler_params=_cparams(("parallel",)),
        name="proj_res",
    )(a, w.astype(BF16), res)


def _rope_tables(pos, half):
    inv = ROPE_BASE ** (-jnp.arange(half, dtype=F32) / half)
    ang = pos.astype(F32)[:, None] * inv[None, :]
    return jnp.cos(ang), jnp.sin(ang)


def _ret_proj_kernel(x_ref, g_ref, w_ref, cos_ref, sin_ref, q_ref, k_ref, v_ref, sg_ref, *, d, nh, rv):
    dk = d // nh
    half = dk // 2
    tm = x_ref.shape[0]
    ts = min(tm, 256)
    norm = lambda r: _rmsnorm(x_ref[pl.ds(r * ts, ts), :], g_ref[...]).astype(BF16)
    xn = norm(0)
    for r in range(tm // ts):
        rows = pl.ds(r * ts, ts)
        cos = cos_ref[rows, :]
        sin = sin_ref[rows, :]
        sg_ref[rows, :] = _silu(_dot(xn, w_ref[:, 2 * d + rv:2 * d + 2 * rv])).astype(BF16)
        xn_next = norm(r + 1) if r + 1 < tm // ts else None
        for off, out_ref, scale in ((0, q_ref, dk ** -0.5), (d, k_ref, 1.0)):
            t = _dot(xn, w_ref[:, off:off + d])
            for hd in range(nh):
                x1 = t[:, hd * dk:hd * dk + half]
                x2 = t[:, hd * dk + half:(hd + 1) * dk]
                out_ref[rows, hd * dk:hd * dk + half] = ((x1 * cos - x2 * sin) * scale).astype(BF16)
                out_ref[rows, hd * dk + half:(hd + 1) * dk] = ((x1 * sin + x2 * cos) * scale).astype(BF16)
        v_ref[rows, :] = _dot(xn, w_ref[:, 2 * d:2 * d + rv]).astype(BF16)
        xn = xn_next


def _ret_proj(h, g, w_in, pos, seq_len, nh, rv):
    n, d = h.shape
    half = d // nh // 2
    tm = _row_tile(n, 512)
    cos, sin = _rope_tables(pos, half)
    if seq_len % tm == 0:
        period = seq_len // tm
    else:
        assert tm % seq_len == 0
        cos, sin = (jnp.tile(t, (tm // seq_len, 1)) for t in (cos, sin))
        period = 1
    tab = pl.BlockSpec((tm, half), lambda i: (i % period, 0))
    row = lambda w: pl.BlockSpec((tm, w), lambda i: (i, 0))
    return pl.pallas_call(
        functools.partial(_ret_proj_kernel, d=d, nh=nh, rv=rv),
        out_shape=(jax.ShapeDtypeStruct((n, d), BF16), jax.ShapeDtypeStruct((n, d), BF16),
                   jax.ShapeDtypeStruct((n, rv), BF16), jax.ShapeDtypeStruct((n, rv), BF16)),
        grid=(n // tm,),
        in_specs=[row(d), _resident((1, d)), _resident((d, 2 * d + 2 * rv)), tab, tab],
        out_specs=(row(d), row(d), row(rv), row(rv)),
        compiler_params=_cparams(("parallel",)),
        name="ret_proj",
    )(h, g.reshape(1, d), w_in.astype(BF16), cos, sin)


def _ret_decay_tables(nh, t, chunk):
    log_gamma = jnp.log1p(-jnp.exp2(-5.0 - jnp.arange(nh, dtype=F32)))
    pos = jnp.arange(t, dtype=F32)
    dist = jnp.abs(pos[:, None] - pos[None, :])
    cid = jnp.arange(t) // chunk
    visible = (cid[None, :] <= cid[:, None]).astype(F32)
    dmask = jnp.exp(log_gamma[:, None, None] * dist) * visible[None]
    qdec = jnp.exp(log_gamma[:, None] * (pos[None, :] + 1.0))[:, :, None]
    kdec = jnp.exp(log_gamma[:, None] * (t - 1.0 - pos[None, :]))[:, :, None]
    sdec = jnp.exp(log_gamma * t)
    return dmask, qdec, kdec, sdec


def _retention_kernel(sdec_ref, q_ref, k_ref, v_ref, sg_ref, dmask_ref, qdec_ref, kdec_ref, gn_ref, s0_ref,
                      o_ref, sout_ref, s_ref, *, nh, dk, dv):
    j = pl.program_id(1)

    @pl.when(j == 0)
    def _():
        s_ref[...] = s0_ref[...]

    for hd in range(nh):
        qh = q_ref[:, hd * dk:(hd + 1) * dk]
        kh = k_ref[:, hd * dk:(hd + 1) * dk]
        vh = v_ref[:, hd * dv:(hd + 1) * dv]
        state = s_ref[hd]
        scores = _dot_nt(qh, kh) * dmask_ref[hd]
        o = _dot(scores.astype(BF16), vh) + _dot(qh, state.astype(BF16)) * qdec_ref[hd]
        kd = (kh.astype(F32) * kdec_ref[hd]).astype(BF16)
        s_ref[hd] = sdec_ref[hd] * state + _dot_tn(kd, vh)
        mu = jnp.mean(o, axis=-1, keepdims=True)
        oc = o - mu
        var = jnp.mean(oc * oc, axis=-1, keepdims=True)
        on = oc * lax.rsqrt(var + EPS) * gn_ref[:, hd * dv:(hd + 1) * dv]
        o_ref[:, hd * dv:(hd + 1) * dv] = (sg_ref[:, hd * dv:(hd + 1) * dv].astype(F32) * on).astype(BF16)

    @pl.when(j == pl.num_programs(1) - 1)
    def _():
        sout_ref[...] = s_ref[...]


def _retention(q, k, v, sg, s0, gn_g, batch, seq_len, chunk):
    nh, dk, dv = s0.shape[1:]
    rv = nh * dv
    t = _row_tile(seq_len, max(chunk, 256))
    assert t % chunk == 0
    nblk = seq_len // t
    dmask, qdec, kdec, sdec = _ret_decay_tables(nh, t, chunk)
    row = lambda w: pl.BlockSpec((t, w), lambda b, j: (b * nblk + j, 0))
    st = pl.BlockSpec((None, nh, dk, dv), lambda b, j: (b, 0, 0, 0))
    return pl.pallas_call(
        functools.partial(_retention_kernel, nh=nh, dk=dk, dv=dv),
        out_shape=(jax.ShapeDtypeStruct((batch * seq_len, rv), BF16),
                   jax.ShapeDtypeStruct((batch, nh, dk, dv), F32)),
        grid=(batch, nblk),
        in_specs=[
            pl.BlockSpec(memory_space=pltpu.SMEM),
            row(nh * dk), row(nh * dk), row(rv), row(rv),
            _resident((nh, t, t)), _resident((nh, t, 1)), _resident((nh, t, 1)), _resident((1, rv)),
            st,
        ],
        out_specs=(row(rv), st),
        scratch_shapes=[pltpu.VMEM((nh, dk, dv), F32)],
        compiler_params=_cparams(("parallel", "arbitrary")),
        name="retention",
    )(sdec, q, k, v, sg, dmask, qdec, kdec, gn_g.reshape(1, rv), s0)


def _lane_iota(rows):
    return lax.broadcasted_iota(jnp.int32, (rows, LANES), 1)


def _head_lanes(x, hd):
    t = x[:, (hd // 2) * LANES:(hd // 2 + 1) * LANES]
    return pltpu.roll(t, HALF_LANES, axis=1) if hd % 2 else t


def _tri_cumsum(tri, x):
    hi, mid, lo = _split3(x)
    return _dot(tri, hi.astype(BF16)) + _dot(tri, mid.astype(BF16)) + _dot(tri, lo.astype(BF16))


def _pack_keys(k, c, ka_ref, nh):
    rows = k.shape[0]
    lane = _lane_iota(rows)
    hi, mid, lo = _split3(c * LOG2E)
    extra = jnp.where(lane < AUG_QPIECE0 + N_PIECES, 1.0, 0.0)
    for p, piece in reversed(list(enumerate((hi, mid, lo)))):
        lo_lane = AUG_PIECE0 + p * nh
        extra = jnp.where(lane < lo_lane + nh, -pltpu.roll(piece, lo_lane, axis=1), extra)
    for hd in range(nh):
        ka_ref[hd] = jnp.where(lane < HALF_LANES, _head_lanes(k, hd), extra).astype(BF16)


def _pack_values(v, vt_ref, nh, tk):
    rows = v.shape[0]
    vt = v.T
    sub = lax.broadcasted_iota(jnp.int32, (VT_ROWS - HALF_LANES, rows), 0)
    ones_row = jnp.where(sub == 0, 1.0, 0.0)
    for hd in range(nh):
        t = jnp.concatenate([vt[hd * HALF_LANES:(hd + 1) * HALF_LANES, :], ones_row], axis=0).astype(BF16)
        for s in range(rows // tk):
            vt_ref[hd, s] = t[:, s * tk:(s + 1) * tk]


def _fox_kv_kernel(x_ref, g_ref, wk_ref, wv_ref, wf_ref, bf_ref, tri_ref, k_ref, v_ref, logf_ref, *rest,
                   nh, tk, packed):
    tm = x_ref.shape[0]
    ts = tri_ref.shape[0]
    if packed:
        ka_ref, vt_ref, ct_ref, carry_ref = rest

        @pl.when(pl.program_id(1) == 0)
        def _():
            carry_ref[...] = jnp.zeros_like(carry_ref)

    def project(r):
        rows = pl.ds(r * ts, ts)
        xn = _rmsnorm(x_ref[rows, :], g_ref[...]).astype(BF16)
        k = _dot(xn, wk_ref[...])
        v = _dot(xn, wv_ref[...])
        z = _dot(xn, wf_ref[...]) + bf_ref[...]
        return k, v, jnp.minimum(z, 0.0) - jnp.log1p(jnp.exp(-jnp.abs(z)))

    def emit(r, k, v, logf):
        logf_ref[pl.ds(r * ts, ts), :] = logf
        for src, dst in ((k, k_ref), (v, v_ref)):
            for hd in range(nh):
                dst[pl.ds(r * ts * nh + hd, ts, stride=nh), :] = _head_lanes(src, hd)[:, :HALF_LANES]
        if packed:
            c = _tri_cumsum(tri_ref[...], jnp.where(_lane_iota(ts) < nh, logf, 0.0)) + carry_ref[...]
            carry_ref[...] = c[ts - 1:ts, :]
            ct_ref[:, pl.ds(r * ts, ts)] = c.T[:nh, :]
            _pack_keys(k, c, ka_ref.at[:, pl.ds(r * ts, ts), :], nh)
            _pack_values(v, vt_ref.at[:, pl.ds(r * (ts // tk), ts // tk)], nh, tk)

    pending = project(0)
    for r in range(tm // ts):
        done = pending
        if r + 1 < tm // ts:
            pending = project(r + 1)
        emit(r, *done)


def _fox_kv(h, g, w_kvf, b_f, width, nh, batch, seq_len, tk=None):
    n, d = h.shape
    packed = tk is not None
    tm = _row_tile(seq_len if packed else n, 512)
    grid = (batch, seq_len // tm) if packed else (n // tm, 1)
    nblk = grid[1]
    assert width == nh * HALF_LANES and N_PIECES * nh <= AUG_QPIECE0 - AUG_PIECE0
    assert nh & (nh - 1) == 0 and AUG_PIECE0 % nh == 0
    wf = jnp.zeros((d, LANES), BF16).at[:, :nh].set(w_kvf[:, 2 * width:].astype(BF16))
    bf = jnp.zeros((1, LANES), F32).at[0, :nh].set(b_f)
    ts = min(tm, 256)
    assert tm % ts == 0
    tri = (jnp.arange(ts)[:, None] >= jnp.arange(ts)[None, :]).astype(BF16)
    row = lambda r, w: pl.BlockSpec((r, w), lambda i, j: (i * nblk + j, 0))
    out_shape = [jax.ShapeDtypeStruct((n * nh, HALF_LANES), F32), jax.ShapeDtypeStruct((n * nh, HALF_LANES), F32),
                 jax.ShapeDtypeStruct((n, LANES), F32)]
    out_specs = [row(tm * nh, HALF_LANES), row(tm * nh, HALF_LANES), row(tm, LANES)]
    scratch = []
    if packed:
        assert ts % tk == 0
        out_shape += [jax.ShapeDtypeStruct((batch, nh, seq_len, LANES), BF16),
                      jax.ShapeDtypeStruct((batch, nh, seq_len // tk, VT_ROWS, tk), BF16),
                      jax.ShapeDtypeStruct((batch, nh, seq_len), F32)]
        out_specs += [pl.BlockSpec((None, nh, tm, LANES), lambda i, j: (i, 0, j, 0)),
                      pl.BlockSpec((None, nh, tm // tk, VT_ROWS, tk), lambda i, j: (i, 0, j, 0, 0)),
                      pl.BlockSpec((None, nh, tm), lambda i, j: (i, 0, j))]
        scratch = [pltpu.VMEM((1, LANES), F32)]
    return pl.pallas_call(
        functools.partial(_fox_kv_kernel, nh=nh, tk=tk, packed=packed),
        out_shape=tuple(out_shape),
        grid=grid,
        in_specs=[row(tm, d), _resident((1, d)), _resident((d, width)), _resident((d, width)),
                  _resident((d, LANES)), _resident((1, LANES)), _resident((ts, ts))],
        out_specs=tuple(out_specs),
        scratch_shapes=scratch,
        compiler_params=_cparams(("parallel", "arbitrary")),
        name="fox_kv",
    )(h, g.reshape(1, d), w_kvf[:, :width].astype(BF16), w_kvf[:, width:2 * width].astype(BF16), wf, bf, tri)


def _cumsum_kernel(x_ref, tri_ref, o_ref, *, tc, nblk):
    tri = tri_ref[...]

    def body(i, carry):
        r0 = pl.multiple_of(i * tc, tc)
        c = _tri_cumsum(tri, x_ref[pl.ds(r0, tc), :]) + carry
        o_ref[pl.ds(r0, tc), :] = c
        return c[tc - 1:tc, :]

    lax.fori_loop(0, nblk, body, jnp.zeros((1, x_ref.shape[-1]), F32))


def _cumsum_rows(x):
    b, n, w = x.shape
    tc = 128 if n % 128 == 0 else 64
    assert n % tc == 0
    tri = (jnp.arange(tc)[:, None] >= jnp.arange(tc)[None, :]).astype(BF16)
    blk = pl.BlockSpec((None, n, w), lambda i: (i, 0, 0))
    return pl.pallas_call(
        functools.partial(_cumsum_kernel, tc=tc, nblk=n // tc),
        out_shape=jax.ShapeDtypeStruct((b, n, w), F32),
        grid=(b,),
        in_specs=[blk, _resident((tc, tc))],
        out_specs=blk,
        compiler_params=_cparams(("parallel",)),
        name="cumsum",
    )(x, tri)


def _fox_attn_cached_kernel(q_ref, k_ref, v_ref, c_ref, o_ref, ka_ref, vt_ref, *, nh, q_off):
    n = k_ref.shape[0]
    tq = q_ref.shape[-1]
    _pack_keys(k_ref[...], c_ref[...], ka_ref, nh)
    _pack_values(v_ref[...], vt_ref, nh, n)
    kpos = lax.broadcasted_iota(jnp.int32, (n, tq), 0)
    qpos = lax.broadcasted_iota(jnp.int32, (n, tq), 1)
    visible = qpos + q_off >= kpos
    scores = lambda hh: _dot(ka_ref[hh], q_ref[hh])
    halves = []
    pending = scores(0)
    for hh in range(nh):
        s = jnp.where(visible, pending, -jnp.inf)
        if hh + 1 < nh:
            pending = scores(hh + 1)
        p = jnp.exp2(s - jnp.max(s, axis=0, keepdims=True)).astype(BF16)
        a = _dot(vt_ref[hh, 0], p)
        halves.append(a[:HALF_LANES, :] / a[HALF_LANES:HALF_LANES + 1, :])
        if hh % 2:
            o_ref[:, (hh // 2) * LANES:(hh // 2 + 1) * LANES] = jnp.concatenate(halves, axis=0).T.astype(BF16)
            halves = []


def _fox_attn_cached(qa, k, v, c, q_off, nh):
    b, n, width = k.shape
    tq = qa.shape[-1]
    assert n % LANES == 0 and tq % LANES == 0 and q_off + tq <= n
    blk = lambda w: pl.BlockSpec((None, n, w), lambda i: (i, 0, 0))
    return pl.pallas_call(
        functools.partial(_fox_attn_cached_kernel, nh=nh, q_off=q_off),
        out_shape=jax.ShapeDtypeStruct((b * tq, width), BF16),
        grid=(b,),
        in_specs=[pl.BlockSpec((None, nh, LANES, tq), lambda i: (i, 0, 0, 0)), blk(width), blk(width), blk(LANES)],
        out_specs=pl.BlockSpec((tq, width), lambda i: (i, 0)),
        scratch_shapes=[pltpu.VMEM((nh, n, LANES), BF16), pltpu.VMEM((nh, 1, VT_ROWS, n), BF16)],
        compiler_params=_cparams(("parallel",)),
        name="fox_attn_cached",
    )(qa, k, v, c)


def _fox_q_kernel(x_ref, g_ref, wt_ref, ct_ref, qa_ref, *, nh):
    tm = x_ref.shape[0]
    xn = _rmsnorm(x_ref[...], g_ref[...]).astype(BF16)
    qt = _dot_nt(wt_ref[...], xn) * (HALF_LANES ** -0.5 * LOG2E)
    hi, mid, lo = _split3(ct_ref[...] * LOG2E)
    sub = lax.broadcasted_iota(jnp.int32, (8, tm), 0)
    rowp = lax.broadcasted_iota(jnp.int32, (N_PIECES * nh, tm), 0)
    tail = jnp.zeros((LANES - AUG_QPIECE0 - 8, tm), F32)
    for hd in range(nh):
        onehot = jnp.where((rowp & (nh - 1)) == hd, 1.0, 0.0)
        own = [jnp.broadcast_to(a[hd:hd + 1, :], (8, tm)) for a in (hi, mid, lo)]
        pieces = jnp.where(sub == 0, own[0], jnp.where(sub == 1, own[1], jnp.where(sub == 2, own[2], 0.0)))
        tile = jnp.concatenate([qt[hd * HALF_LANES:(hd + 1) * HALF_LANES, :], onehot, pieces, tail], axis=0)
        qa_ref[hd] = tile.astype(BF16)


def _fox_q(h, g, w_q, cq_t, batch, seq_len, nh):
    n, d = h.shape
    tm = _row_tile(seq_len, 1024)
    nblk = seq_len // tm
    return pl.pallas_call(
        functools.partial(_fox_q_kernel, nh=nh),
        out_shape=jax.ShapeDtypeStruct((batch, nh, LANES, seq_len), BF16),
        grid=(batch, nblk),
        in_specs=[
            pl.BlockSpec((tm, d), lambda b, j: (b * nblk + j, 0)),
            _resident((1, d)),
            _resident((nh * HALF_LANES, d)),
            pl.BlockSpec((None, nh, tm), lambda b, j: (b, 0, j)),
        ],
        out_specs=pl.BlockSpec((None, nh, LANES, tm), lambda b, j: (b, 0, 0, j)),
        compiler_params=_cparams(("parallel", "parallel")),
        name="fox_q",
    )(h, g.reshape(1, d), w_q.T.astype(BF16), cq_t)


def _fox_attn_kernel(q_ref, k_ref, v_ref, o_ref, s0_ref, s1_ref, acc_ref, m_ref, *, hg, tq, tk, q_off,
                     single_block):
    i = pl.program_id(2)
    q0 = q_off + i * tq
    n_full, n_mask = (0, 1) if single_block else (q0 // tk, tq // tk)
    acc_ref[...] = jnp.zeros_like(acc_ref)
    m_ref[...] = jnp.full_like(m_ref, -jnp.inf)
    slots = (s0_ref, s1_ref)
    cw = min(tq, 256)

    def scores_head(hh, j, slot, c0=0, c1=tq):
        r0 = pl.multiple_of(j * tk, tk)
        slots[slot][hh, :, c0:c1] = _dot(k_ref[hh, pl.ds(r0, tk), :], q_ref[hh, :, c0:c1])

    def absorb_head(hh, j, slot, c0=0, c1=tq, diag=None):
        w = c1 - c0
        s = slots[slot][hh, :, c0:c1]
        if diag is not None:
            kpos = lax.broadcasted_iota(jnp.int32, (tk, w), 0)
            qpos = lax.broadcasted_iota(jnp.int32, (tk, w), 1)
            s = jnp.where(qpos + diag >= kpos, s, -jnp.inf)
        m_prev = m_ref[hh, :, c0:c1]
        m_new = jnp.maximum(m_prev, jnp.max(s, axis=0, keepdims=True))
        alpha = jnp.exp2(m_prev - m_new)
        p = jnp.exp2(s - m_new).astype(BF16)
        acc_ref[hh, :, c0:c1] = alpha * acc_ref[hh, :, c0:c1] + _dot(v_ref[hh, j], p)
        m_ref[hh, :, c0:c1] = m_new

    def stage(j, slot):
        for hh in range(hg):
            for c0 in range(0, tq, cw):
                scores_head(hh, j + 1, 1 - slot, c0, c0 + cw)
                absorb_head(hh, j, slot, c0, c0 + cw)

    for hh in range(hg):
        scores_head(hh, 0, 0)

    def quad(jj, carry):
        for u in range(4):
            stage(4 * jj + u, u % 2)
        return carry

    lax.fori_loop(0, n_full // 4, quad, 0)

    @pl.when(n_full % 4 == 2)
    def _():
        stage(n_full - 2, 0)
        stage(n_full - 1, 1)
    for t in range(n_mask):
        lo = 0 if single_block else t * tk
        hi = min(lo + tk, tq)
        for hh in range(hg):
            if t + 1 < n_mask:
                scores_head(hh, n_full + t + 1, (t + 1) % 2, hi)
            absorb_head(hh, n_full + t, t % 2, lo, hi, diag=q_off if single_block else 0)
            if hi < tq:
                absorb_head(hh, n_full + t, t % 2, hi, tq)
    for g in range(hg // 2):
        halves = []
        for hh in (2 * g, 2 * g + 1):
            a = acc_ref[hh]
            halves.append(a[:HALF_LANES, :] / a[HALF_LANES:HALF_LANES + 1, :])
        o_ref[:, g * LANES:(g + 1) * LANES] = jnp.concatenate(halves, axis=0).T.astype(BF16)


def _fox_attn(qa, ka, vt, q_off):
    b, nh, _, seq_len = qa.shape
    n_keys = ka.shape[2]
    tk = vt.shape[-1]
    hg = 8
    tq = min(seq_len, 512)
    assert seq_len % tq == 0 and tq % LANES == 0 and nh % hg == 0 and q_off + seq_len <= n_keys
    nq = seq_len // tq
    single_block = n_keys == tk
    assert (single_block and nq == 1) or (tq % (2 * tk) == 0 and q_off % (2 * tk) == 0)
    return pl.pallas_call(
        functools.partial(_fox_attn_kernel, hg=hg, tq=tq, tk=tk, q_off=q_off, single_block=single_block),
        out_shape=jax.ShapeDtypeStruct((b * seq_len, nh * HALF_LANES), BF16),
        grid=(b, nh // hg, nq),
        in_specs=[
            pl.BlockSpec((None, hg, LANES, tq), lambda bi, g, i: (bi, g, 0, i)),
            pl.BlockSpec((None, hg, n_keys, LANES), lambda bi, g, i: (bi, g, 0, 0)),
            pl.BlockSpec((None, hg, n_keys // tk, VT_ROWS, tk), lambda bi, g, i: (bi, g, 0, 0, 0)),
        ],
        out_specs=pl.BlockSpec((tq, hg * HALF_LANES), lambda bi, g, i: (bi * nq + i, g)),
        scratch_shapes=[pltpu.VMEM((hg, tk, tq), F32), pltpu.VMEM((hg, tk, tq), F32),
                        pltpu.VMEM((hg, VT_ROWS, tq), F32), pltpu.VMEM((hg, 1, tq), F32)],
        compiler_params=_cparams(("parallel", "parallel", "arbitrary")),
        name="fox_attn",
    )(qa, ka, vt)


def _trunk(x, pos0, chunk, s0, past, p):
    batch, seq_len, d = x.shape
    nh_ret, dk, dv = s0.shape[2:]
    rv = nh_ret * dv
    nh_fox = p['fox_b_f'].shape[0]
    width = p['fox_w_q'].shape[-1]
    pos = pos0 + jnp.arange(seq_len)
    h = x.reshape(batch * seq_len, d)

    h = _ffn(h, p['ffn1_g'][0], p['ffn1_w_in'][0], p['ffn1_w_out'][0])
    q, k, v, sg = _ret_proj(h, p['mix_g'][0], p['ret_w_in'][0], pos, seq_len, nh_ret, rv)
    o, s_fin = _retention(q, k, v, sg, s0[0], p['ret_gn_g'][0], batch, seq_len, chunk)
    h = _proj_res(o, p['ret_w_out'][0], h)
    h = _ffn(h, p['ffn2_g'][0], p['ffn2_w_in'][0], p['ffn2_w_out'][0])

    head_dim = width // nh_fox
    pad_q = -seq_len % LANES
    if past is None:
        assert pad_q == 0
        q_off = 0
        k_new, v_new, logf_pad, ka, vt, cq_t = _fox_kv(h, p['kv_g'], p['fox_w_kvf'], p['fox_b_f'], width, nh_fox,
                                                       batch, seq_len, tk=256)
    else:
        k_new, v_new, logf_pad = _fox_kv(h, p['kv_g'], p['fox_w_kvf'], p['fox_b_f'], width, nh_fox, batch, seq_len)
    k_new = k_new.reshape(batch, seq_len, nh_fox, head_dim)
    v_new = v_new.reshape(batch, seq_len, nh_fox, head_dim)
    logf_new = logf_pad.reshape(batch, seq_len, LANES)[:, :, :nh_fox]
    if past is not None:
        past_len = past[0].shape[1]
        k_all, v_all = (jnp.concatenate([old.reshape(batch, past_len, width), new.reshape(batch, seq_len, width)],
                                        axis=1) for old, new in ((past[0], k_new), (past[1], v_new)))
        logf_all = jnp.concatenate([past[2], logf_new], axis=1)
        n_keys = past_len + seq_len
        q_off = past_len
        pad_k = -n_keys % LANES
        assert pad_q <= pad_k
        if pad_k:
            k_all, v_all, logf_all = (jnp.pad(t, ((0, 0), (0, pad_k), (0, 0))) for t in (k_all, v_all, logf_all))
        cumf = _cumsum_rows(jnp.pad(logf_all, ((0, 0), (0, 0), (0, LANES - nh_fox))))
        cq_t = jnp.swapaxes(cumf[:, q_off:q_off + seq_len, :nh_fox], 1, 2)

    h = _ffn(h, p['ffn1_g'][1], p['ffn1_w_in'][1], p['ffn1_w_out'][1])
    qa = _fox_q(h, p['mix_g'][1], p['fox_w_q'][0], cq_t, batch, seq_len, nh_fox)
    if pad_q:
        qa = jnp.pad(qa, ((0, 0), (0, 0), (0, 0), (0, pad_q)))
    if past is None:
        o = _fox_attn(qa, ka, vt, q_off)
    else:
        o = _fox_attn_cached(qa, k_all, v_all, cumf, q_off, nh_fox)
    if pad_q:
        o = o.reshape(batch, seq_len + pad_q, width)[:, :seq_len].reshape(batch * seq_len, width)
    h = _proj_res(o, p['fox_w_out'][0], h)
    y = _ffn(h, p['ffn2_g'][1], p['ffn2_w_in'][1], p['ffn2_w_out'][1], final_g=p['final_g'])

    return (y.reshape(batch, seq_len, d), s_fin[None], k_new, v_new, logf_new)


def kernel(x_prompt, x_sample, state_ret, cache_k, cache_v, cache_logf, ffn1_g, ffn1_w_in, ffn1_w_out, mix_g,
           ffn2_g, ffn2_w_in, ffn2_w_out, ret_w_in, ret_gn_g, ret_w_out, kv_g, fox_w_kvf, fox_b_f, fox_w_q,
           fox_w_out, final_g):
    p = {'ffn1_g': ffn1_g, 'ffn1_w_in': ffn1_w_in, 'ffn1_w_out': ffn1_w_out, 'mix_g': mix_g,
         'ffn2_g': ffn2_g, 'ffn2_w_in': ffn2_w_in, 'ffn2_w_out': ffn2_w_out,
         'ret_w_in': ret_w_in, 'ret_gn_g': ret_gn_g, 'ret_w_out': ret_w_out,
         'kv_g': kv_g, 'fox_w_kvf': fox_w_kvf, 'fox_b_f': fox_b_f,
         'fox_w_q': fox_w_q, 'fox_w_out': fox_w_out, 'final_g': final_g}
    assert state_ret.shape[0] == 1 and fox_w_q.shape[0] == 1
    s0_prompt = jnp.zeros((1, x_prompt.shape[0]) + state_ret.shape[2:], F32)
    y_p, s_p, k_p, v_p, f_p = _trunk(x_prompt, 0, RET_CHUNK, s0_prompt, None, p)
    y_s, s_s, k_s, v_s, f_s = _trunk(x_sample, cache_k.shape[1], x_sample.shape[1], state_ret,
                                     (cache_k, cache_v, cache_logf), p)
    return (y_p, y_s, s_p, k_p, v_p, f_p, s_s, k_s, v_s, f_s)
```

```python
import functools

import jax
import jax.numpy as jnp
from jax import lax
from jax.experimental import pallas as pl
from jax.experimental.pallas import tpu as pltpu

F32 = jnp.float32
BF16 = jnp.bfloat16

EPS = 1e-6
ROPE_BASE = 10000.0
RET_CHUNK = 64
LANES = 128
HALF_LANES = LANES // 2
V7X_VMEM_LIMIT = 56 * 1024 * 1024
V7X_MXU_DIM = 256
ROW_TILE = 512
ROW_TILE_WIDE = 1024
ATTN_QUERY_TILE = 512
ATTN_HEADS_PER_STEP = 8

AUG_PIECE0 = 64
AUG_QPIECE0 = 112
N_PIECES = 3
VT_ROWS = 80
LOG2E = 1.4426950408889634


def _cparams(sem):
    return pltpu.CompilerParams(dimension_semantics=sem, vmem_limit_bytes=V7X_VMEM_LIMIT)


def _resident(shape):
    nd = len(shape)
    return pl.BlockSpec(shape, lambda *_: (0,) * nd, pipeline_mode=pl.Buffered(1))


def _rmsnorm(x, g):
    ms = jnp.mean(x * x, axis=-1, keepdims=True)
    return x * lax.rsqrt(ms + EPS) * g


def _silu(x):
    return x * jax.nn.sigmoid(x)


def _dot(a, b):
    return jnp.dot(a, b, preferred_element_type=F32)


def _dot_nt(a, b):
    return lax.dot_general(a, b, (((1,), (1,)), ((), ())), preferred_element_type=F32)


def _dot_tn(a, b):
    return lax.dot_general(a, b, (((0,), (0,)), ((), ())), preferred_element_type=F32)


def _split3(c):
    hi = c.astype(BF16).astype(F32)
    r = c - hi
    mid = r.astype(BF16).astype(F32)
    lo = (r - mid).astype(BF16).astype(F32)
    return hi, mid, lo


def _row_tile(n, want):
    t = min(n, want)
    assert n % t == 0 and t % 8 == 0, (n, t)
    return t


def _ffn_kernel(x_ref, g_ref, win_ref, wout_ref, fg_ref, o_ref, act_ref, *, d_ff, ck, final_norm):
    x = x_ref[...]
    xn = _rmsnorm(x, g_ref[...]).astype(BF16)
    for c in range(d_ff // ck):
        gate = _dot(xn, win_ref[:, c * ck:(c + 1) * ck])
        up = _dot(xn, win_ref[:, d_ff + c * ck:d_ff + (c + 1) * ck])
        act_ref[:, c * ck:(c + 1) * ck] = (_silu(gate) * up).astype(BF16)
    y = x + 0.5 * _dot(act_ref[...], wout_ref[...])
    if final_norm:
        y = _rmsnorm(y, fg_ref[...])
    o_ref[...] = y


def _ffn(h, g, w_in, w_out, final_g=None):
    n, d = h.shape
    d_ff = w_out.shape[0]
    tm = _row_tile(n, ROW_TILE_WIDE)
    ck = V7X_MXU_DIM
    assert d_ff % ck == 0
    fg = jnp.ones((d,), F32) if final_g is None else final_g
    return pl.pallas_call(
        functools.partial(_ffn_kernel, d_ff=d_ff, ck=ck, final_norm=final_g is not None),
        out_shape=jax.ShapeDtypeStruct((n, d), F32),
        grid=(n // tm,),
        in_specs=[
            pl.BlockSpec((tm, d), lambda i: (i, 0)),
            _resident((1, d)),
            _resident((d, 2 * d_ff)),
            _resident((d_ff, d)),
            _resident((1, d)),
        ],
        out_specs=pl.BlockSpec((tm, d), lambda i: (i, 0)),
        scratch_shapes=[pltpu.VMEM((tm, d_ff), BF16)],
        compiler_params=_cparams(("parallel",)),
        name="ffn",
    )(h, g.reshape(1, d), w_in.astype(BF16), w_out.astype(BF16), fg.reshape(1, d))


def _proj_res_kernel(a_ref, w_ref, r_ref, o_ref):
    o_ref[...] = r_ref[...] + _dot(a_ref[...], w_ref[...])


def _proj_res(a, w, res):
    n, k = a.shape
    d = w.shape[1]
    tm = _row_tile(n, ROW_TILE_WIDE)
    return pl.pallas_call(
        _proj_res_kernel,
        out_shape=jax.ShapeDtypeStruct((n, d), F32),
        grid=(n // tm,),
        in_specs=[
            pl.BlockSpec((tm, k), lambda i: (i, 0)),
            _resident((k, d)),
            pl.BlockSpec((tm, d), lambda i: (i, 0)),
        ],
        out_specs=pl.BlockSpec((tm, d), lambda i: (i, 0)),
        compiler_params=_cparams(("parallel",)),
        name="proj_res",
    )(a, w.astype(BF16), res)


def _rope_tables(pos, half):
    inv = ROPE_BASE ** (-jnp.arange(half, dtype=F32) / half)
    ang = pos.astype(F32)[:, None] * inv[None, :]
    return jnp.cos(ang), jnp.sin(ang)


def _ret_proj_kernel(x_ref, g_ref, w_ref, cos_ref, sin_ref, q_ref, k_ref, v_ref, sg_ref, *, d, nh, rv):
    dk = d // nh
    half = dk // 2
    tm = x_ref.shape[0]
    ts = min(tm, V7X_MXU_DIM)
    norm = lambda r: _rmsnorm(x_ref[pl.ds(r * ts, ts), :], g_ref[...]).astype(BF16)
    xn = norm(0)
    for r in range(tm // ts):
        rows = pl.ds(r * ts, ts)
        cos = cos_ref[rows, :]
        sin = sin_ref[rows, :]
        sg_ref[rows, :] = _silu(_dot(xn, w_ref[:, 2 * d + rv:2 * d + 2 * rv])).astype(BF16)
        xn_next = norm(r + 1) if r + 1 < tm // ts else None
        for off, out_ref, scale in ((0, q_ref, dk ** -0.5), (d, k_ref, 1.0)):
            t = _dot(xn, w_ref[:, off:off + d])
            for hd in range(nh):
                x1 = t[:, hd * dk:hd * dk + half]
                x2 = t[:, hd * dk + half:(hd + 1) * dk]
                out_ref[rows, hd * dk:hd * dk + half] = ((x1 * cos - x2 * sin) * scale).astype(BF16)
                out_ref[rows, hd * dk + half:(hd + 1) * dk] = ((x1 * sin + x2 * cos) * scale).astype(BF16)
        v_ref[rows, :] = _dot(xn, w_ref[:, 2 * d:2 * d + rv]).astype(BF16)
        xn = xn_next


def _ret_proj(h, g, w_in, pos, seq_len, nh, rv):
    n, d = h.shape
    half = d // nh // 2
    tm = _row_tile(n, ROW_TILE)
    cos, sin = _rope_tables(pos, half)
    if seq_len % tm == 0:
        period = seq_len // tm
    else:
        assert tm % seq_len == 0
        cos, sin = (jnp.tile(t, (tm // seq_len, 1)) for t in (cos, sin))
        period = 1
    tab = pl.BlockSpec((tm, half), lambda i: (i % period, 0))
    row = lambda w: pl.BlockSpec((tm, w), lambda i: (i, 0))
    return pl.pallas_call(
        functools.partial(_ret_proj_kernel, d=d, nh=nh, rv=rv),
        out_shape=(jax.ShapeDtypeStruct((n, d), BF16), jax.ShapeDtypeStruct((n, d), BF16),
                   jax.ShapeDtypeStruct((n, rv), BF16), jax.ShapeDtypeStruct((n, rv), BF16)),
        grid=(n // tm,),
        in_specs=[row(d), _resident((1, d)), _resident((d, 2 * d + 2 * rv)), tab, tab],
        out_specs=(row(d), row(d), row(rv), row(rv)),
        compiler_params=_cparams(("parallel",)),
        name="ret_proj",
    )(h, g.reshape(1, d), w_in.astype(BF16), cos, sin)


def _ret_decay_tables(nh, t, chunk):
    log_gamma = jnp.log1p(-jnp.exp2(-5.0 - jnp.arange(nh, dtype=F32)))
    pos = jnp.arange(t, dtype=F32)
    dist = jnp.abs(pos[:, None] - pos[None, :])
    cid = jnp.arange(t) // chunk
    visible = (cid[None, :] <= cid[:, None]).astype(F32)
    dmask = jnp.exp(log_gamma[:, None, None] * dist) * visible[None]
    qdec = jnp.exp(log_gamma[:, None] * (pos[None, :] + 1.0))[:, :, None]
    kdec = jnp.exp(log_gamma[:, None] * (t - 1.0 - pos[None, :]))[:, :, None]
    sdec = jnp.exp(log_gamma * t)
    return dmask, qdec, kdec, sdec


def _retention_kernel(sdec_ref, q_ref, k_ref, v_ref, sg_ref, dmask_ref, qdec_ref, kdec_ref, gn_ref, s0_ref,
                      o_ref, sout_ref, s_ref, *, nh, dk, dv):
    j = pl.program_id(1)

    @pl.when(j == 0)
    def _():
        s_ref[...] = s0_ref[...]

    for hd in range(nh):
        qh = q_ref[:, hd * dk:(hd + 1) * dk]
        kh = k_ref[:, hd * dk:(hd + 1) * dk]
        vh = v_ref[:, hd * dv:(hd + 1) * dv]
        state = s_ref[hd]
        scores = _dot_nt(qh, kh) * dmask_ref[hd]
        o = _dot(scores.astype(BF16), vh) + _dot(qh, state.astype(BF16)) * qdec_ref[hd]
        kd = (kh.astype(F32) * kdec_ref[hd]).astype(BF16)
        s_ref[hd] = sdec_ref[hd] * state + _dot_tn(kd, vh)
        mu = jnp.mean(o, axis=-1, keepdims=True)
        oc = o - mu
        var = jnp.mean(oc * oc, axis=-1, keepdims=True)
        on = oc * lax.rsqrt(var + EPS) * gn_ref[:, hd * dv:(hd + 1) * dv]
        o_ref[:, hd * dv:(hd + 1) * dv] = (sg_ref[:, hd * dv:(hd + 1) * dv].astype(F32) * on).astype(BF16)

    @pl.when(j == pl.num_programs(1) - 1)
    def _():
        sout_ref[...] = s_ref[...]


def _retention(q, k, v, sg, s0, gn_g, batch, seq_len, chunk):
    nh, dk, dv = s0.shape[1:]
    rv = nh * dv
    t = _row_tile(seq_len, max(chunk, V7X_MXU_DIM))
    assert t % chunk == 0
    nblk = seq_len // t
    dmask, qdec, kdec, sdec = _ret_decay_tables(nh, t, chunk)
    row = lambda w: pl.BlockSpec((t, w), lambda b, j: (b * nblk + j, 0))
    st = pl.BlockSpec((None, nh, dk, dv), lambda b, j: (b, 0, 0, 0))
    return pl.pallas_call(
        functools.partial(_retention_kernel, nh=nh, dk=dk, dv=dv),
        out_shape=(jax.ShapeDtypeStruct((batch * seq_len, rv), BF16),
                   jax.ShapeDtypeStruct((batch, nh, dk, dv), F32)),
        grid=(batch, nblk),
        in_specs=[
            pl.BlockSpec(memory_space=pltpu.SMEM),
            row(nh * dk), row(nh * dk), row(rv), row(rv),
            _resident((nh, t, t)), _resident((nh, t, 1)), _resident((nh, t, 1)), _resident((1, rv)),
            st,
        ],
        out_specs=(row(rv), st),
        scratch_shapes=[pltpu.VMEM((nh, dk, dv), F32)],
        compiler_params=_cparams(("parallel", "arbitrary")),
        name="retention",
    )(sdec, q, k, v, sg, dmask, qdec, kdec, gn_g.reshape(1, rv), s0)


def _lane_iota(rows):
    return lax.broadcasted_iota(jnp.int32, (rows, LANES), 1)


def _head_lanes(x, hd):
    t = x[:, (hd // 2) * LANES:(hd // 2 + 1) * LANES]
    return pltpu.roll(t, HALF_LANES, axis=1) if hd % 2 else t


def _tri_cumsum(tri, x):
    hi, mid, lo = _split3(x)
    return _dot(tri, hi.astype(BF16)) + _dot(tri, mid.astype(BF16)) + _dot(tri, lo.astype(BF16))


def _pack_keys(k, c, ka_ref, nh):
    rows = k.shape[0]
    lane = _lane_iota(rows)
    hi, mid, lo = _split3(c * LOG2E)
    extra = jnp.where(lane < AUG_QPIECE0 + N_PIECES, 1.0, 0.0)
    for p, piece in reversed(list(enumerate((hi, mid, lo)))):
        lo_lane = AUG_PIECE0 + p * nh
        extra = jnp.where(lane < lo_lane + nh, -pltpu.roll(piece, lo_lane, axis=1), extra)
    for hd in range(nh):
        ka_ref[hd] = jnp.where(lane < HALF_LANES, _head_lanes(k, hd), extra).astype(BF16)


def _pack_values(v, vt_ref, nh, tk):
    rows = v.shape[0]
    vt = v.T
    sub = lax.broadcasted_iota(jnp.int32, (VT_ROWS - HALF_LANES, rows), 0)
    ones_row = jnp.where(sub == 0, 1.0, 0.0)
    for hd in range(nh):
        t = jnp.concatenate([vt[hd * HALF_LANES:(hd + 1) * HALF_LANES, :], ones_row], axis=0).astype(BF16)
        for s in range(rows // tk):
            vt_ref[hd, s] = t[:, s * tk:(s + 1) * tk]


def _fox_kv_kernel(x_ref, g_ref, wk_ref, wv_ref, wf_ref, bf_ref, tri_ref, k_ref, v_ref, logf_ref, *rest,
                   nh, tk, packed):
    tm = x_ref.shape[0]
    ts = tri_ref.shape[0]
    if packed:
        ka_ref, vt_ref, ct_ref, carry_ref = rest

        @pl.when(pl.program_id(1) == 0)
        def _():
            carry_ref[...] = jnp.zeros_like(carry_ref)

    def project(r):
        rows = pl.ds(r * ts, ts)
        xn = _rmsnorm(x_ref[rows, :], g_ref[...]).astype(BF16)
        k = _dot(xn, wk_ref[...])
        v = _dot(xn, wv_ref[...])
        z = _dot(xn, wf_ref[...]) + bf_ref[...]
        return k, v, jnp.minimum(z, 0.0) - jnp.log1p(jnp.exp(-jnp.abs(z)))

    def emit(r, k, v, logf):
        logf_ref[pl.ds(r * ts, ts), :] = logf
        for src, dst in ((k, k_ref), (v, v_ref)):
            for hd in range(nh):
                dst[pl.ds(r * ts * nh + hd, ts, stride=nh), :] = _head_lanes(src, hd)[:, :HALF_LANES]
        if packed:
            c = _tri_cumsum(tri_ref[...], jnp.where(_lane_iota(ts) < nh, logf, 0.0)) + carry_ref[...]
            carry_ref[...] = c[ts - 1:ts, :]
            ct_ref[:, pl.ds(r * ts, ts)] = c.T[:nh, :]
            _pack_keys(k, c, ka_ref.at[:, pl.ds(r * ts, ts), :], nh)
            _pack_values(v, vt_ref.at[:, pl.ds(r * (ts // tk), ts // tk)], nh, tk)

    pending = project(0)
    for r in range(tm // ts):
        done = pending
        if r + 1 < tm // ts:
            pending = project(r + 1)
        emit(r, *done)


def _fox_kv(h, g, w_kvf, b_f, width, nh, batch, seq_len, tk=None):
    n, d = h.shape
    packed = tk is not None
    tm = _row_tile(seq_len if packed else n, ROW_TILE)
    grid = (batch, seq_len // tm) if packed else (n // tm, 1)
    nblk = grid[1]
    assert width == nh * HALF_LANES and N_PIECES * nh <= AUG_QPIECE0 - AUG_PIECE0
    assert nh & (nh - 1) == 0 and AUG_PIECE0 % nh == 0
    wf = jnp.zeros((d, LANES), BF16).at[:, :nh].set(w_kvf[:, 2 * width:].astype(BF16))
    bf = jnp.zeros((1, LANES), F32).at[0, :nh].set(b_f)
    ts = min(tm, V7X_MXU_DIM)
    assert tm % ts == 0
    tri = (jnp.arange(ts)[:, None] >= jnp.arange(ts)[None, :]).astype(BF16)
    row = lambda r, w: pl.BlockSpec((r, w), lambda i, j: (i * nblk + j, 0))
    out_shape = [jax.ShapeDtypeStruct((n * nh, HALF_LANES), F32), jax.ShapeDtypeStruct((n * nh, HALF_LANES), F32),
                 jax.ShapeDtypeStruct((n, LANES), F32)]
    out_specs = [row(tm * nh, HALF_LANES), row(tm * nh, HALF_LANES), row(tm, LANES)]
    scratch = []
    if packed:
        assert ts % tk == 0
        out_shape += [jax.ShapeDtypeStruct((batch, nh, seq_len, LANES), BF16),
                      jax.ShapeDtypeStruct((batch, nh, seq_len // tk, VT_ROWS, tk), BF16),
                      jax.ShapeDtypeStruct((batch, nh, seq_len), F32)]
        out_specs += [pl.BlockSpec((None, nh, tm, LANES), lambda i, j: (i, 0, j, 0)),
                      pl.BlockSpec((None, nh, tm // tk, VT_ROWS, tk), lambda i, j: (i, 0, j, 0, 0)),
                      pl.BlockSpec((None, nh, tm), lambda i, j: (i, 0, j))]
        scratch = [pltpu.VMEM((1, LANES), F32)]
    return pl.pallas_call(
        functools.partial(_fox_kv_kernel, nh=nh, tk=tk, packed=packed),
        out_shape=tuple(out_shape),
        grid=grid,
        in_specs=[row(tm, d), _resident((1, d)), _resident((d, width)), _resident((d, width)),
                  _resident((d, LANES)), _resident((1, LANES)), _resident((ts, ts))],
        out_specs=tuple(out_specs),
        scratch_shapes=scratch,
        compiler_params=_cparams(("parallel", "arbitrary")),
        name="fox_kv",
    )(h, g.reshape(1, d), w_kvf[:, :width].astype(BF16), w_kvf[:, width:2 * width].astype(BF16), wf, bf, tri)


def _cumsum_kernel(x_ref, tri_ref, o_ref, *, tc, nblk):
    tri = tri_ref[...]

    def body(i, carry):
        r0 = pl.multiple_of(i * tc, tc)
        c = _tri_cumsum(tri, x_ref[pl.ds(r0, tc), :]) + carry
        o_ref[pl.ds(r0, tc), :] = c
        return c[tc - 1:tc, :]

    lax.fori_loop(0, nblk, body, jnp.zeros((1, x_ref.shape[-1]), F32))


def _cumsum_rows(x):
    b, n, w = x.shape
    tc = LANES if n % LANES == 0 else HALF_LANES
    assert n % tc == 0
    tri = (jnp.arange(tc)[:, None] >= jnp.arange(tc)[None, :]).astype(BF16)
    blk = pl.BlockSpec((None, n, w), lambda i: (i, 0, 0))
    return pl.pallas_call(
        functools.partial(_cumsum_kernel, tc=tc, nblk=n // tc),
        out_shape=jax.ShapeDtypeStruct((b, n, w), F32),
        grid=(b,),
        in_specs=[blk, _resident((tc, tc))],
        out_specs=blk,
        compiler_params=_cparams(("parallel",)),
        name="cumsum",
    )(x, tri)


def _fox_attn_cached_kernel(q_ref, k_ref, v_ref, c_ref, o_ref, ka_ref, vt_ref, *, nh, q_off):
    n = k_ref.shape[0]
    tq = q_ref.shape[-1]
    _pack_keys(k_ref[...], c_ref[...], ka_ref, nh)
    _pack_values(v_ref[...], vt_ref, nh, n)
    kpos = lax.broadcasted_iota(jnp.int32, (n, tq), 0)
    qpos = lax.broadcasted_iota(jnp.int32, (n, tq), 1)
    visible = qpos + q_off >= kpos
    scores = lambda hh: _dot(ka_ref[hh], q_ref[hh])
    halves = []
    pending = scores(0)
    for hh in range(nh):
        s = jnp.where(visible, pending, -jnp.inf)
        if hh + 1 < nh:
            pending = scores(hh + 1)
        p = jnp.exp2(s - jnp.max(s, axis=0, keepdims=True)).astype(BF16)
        a = _dot(vt_ref[hh, 0], p)
        halves.append(a[:HALF_LANES, :] / a[HALF_LANES:HALF_LANES + 1, :])
        if hh % 2:
            o_ref[:, (hh // 2) * LANES:(hh // 2 + 1) * LANES] = jnp.concatenate(halves, axis=0).T.astype(BF16)
            halves = []


def _fox_attn_cached(qa, k, v, c, q_off, nh):
    b, n, width = k.shape
    tq = qa.shape[-1]
    assert n % LANES == 0 and tq % LANES == 0 and q_off + tq <= n
    blk = lambda w: pl.BlockSpec((None, n, w), lambda i: (i, 0, 0))
    return pl.pallas_call(
        functools.partial(_fox_attn_cached_kernel, nh=nh, q_off=q_off),
        out_shape=jax.ShapeDtypeStruct((b * tq, width), BF16),
        grid=(b,),
        in_specs=[pl.BlockSpec((None, nh, LANES, tq), lambda i: (i, 0, 0, 0)), blk(width), blk(width), blk(LANES)],
        out_specs=pl.BlockSpec((tq, width), lambda i: (i, 0)),
        scratch_shapes=[pltpu.VMEM((nh, n, LANES), BF16), pltpu.VMEM((nh, 1, VT_ROWS, n), BF16)],
        compiler_params=_cparams(("parallel",)),
        name="fox_attn_cached",
    )(qa, k, v, c)


def _fox_q_kernel(x_ref, g_ref, wt_ref, ct_ref, qa_ref, *, nh):
    tm = x_ref.shape[0]
    xn = _rmsnorm(x_ref[...], g_ref[...]).astype(BF16)
    qt = _dot_nt(wt_ref[...], xn) * (HALF_LANES ** -0.5 * LOG2E)
    hi, mid, lo = _split3(ct_ref[...] * LOG2E)
    sub = lax.broadcasted_iota(jnp.int32, (8, tm), 0)
    rowp = lax.broadcasted_iota(jnp.int32, (N_PIECES * nh, tm), 0)
    tail = jnp.zeros((LANES - AUG_QPIECE0 - 8, tm), F32)
    for hd in range(nh):
        onehot = jnp.where((rowp & (nh - 1)) == hd, 1.0, 0.0)
        own = [jnp.broadcast_to(a[hd:hd + 1, :], (8, tm)) for a in (hi, mid, lo)]
        pieces = jnp.where(sub == 0, own[0], jnp.where(sub == 1, own[1], jnp.where(sub == 2, own[2], 0.0)))
        tile = jnp.concatenate([qt[hd * HALF_LANES:(hd + 1) * HALF_LANES, :], onehot, pieces, tail], axis=0)
        qa_ref[hd] = tile.astype(BF16)


def _fox_q(h, g, w_q, cq_t, batch, seq_len, nh):
    n, d = h.shape
    tm = _row_tile(seq_len, ROW_TILE_WIDE)
    nblk = seq_len // tm
    return pl.pallas_call(
        functools.partial(_fox_q_kernel, nh=nh),
        out_shape=jax.ShapeDtypeStruct((batch, nh, LANES, seq_len), BF16),
        grid=(batch, nblk),
        in_specs=[
            pl.BlockSpec((tm, d), lambda b, j: (b * nblk + j, 0)),
            _resident((1, d)),
            _resident((nh * HALF_LANES, d)),
            pl.BlockSpec((None, nh, tm), lambda b, j: (b, 0, j)),
        ],
        out_specs=pl.BlockSpec((None, nh, LANES, tm), lambda b, j: (b, 0, 0, j)),
        compiler_params=_cparams(("parallel", "parallel")),
        name="fox_q",
    )(h, g.reshape(1, d), w_q.T.astype(BF16), cq_t)


def _fox_attn_kernel(q_ref, k_ref, v_ref, o_ref, s0_ref, s1_ref, acc_ref, m_ref, *, hg, tq, tk, q_off,
                     single_block):
    i = pl.program_id(2)
    q0 = q_off + i * tq
    n_full, n_mask = (0, 1) if single_block else (q0 // tk, tq // tk)
    acc_ref[...] = jnp.zeros_like(acc_ref)
    m_ref[...] = jnp.full_like(m_ref, -jnp.inf)
    slots = (s0_ref, s1_ref)
    cw = min(tq, V7X_MXU_DIM)

    def scores_head(hh, j, slot, c0=0, c1=tq):
        r0 = pl.multiple_of(j * tk, tk)
        slots[slot][hh, :, c0:c1] = _dot(k_ref[hh, pl.ds(r0, tk), :], q_ref[hh, :, c0:c1])

    def absorb_head(hh, j, slot, c0=0, c1=tq, diag=None):
        w = c1 - c0
        s = slots[slot][hh, :, c0:c1]
        if diag is not None:
            kpos = lax.broadcasted_iota(jnp.int32, (tk, w), 0)
            qpos = lax.broadcasted_iota(jnp.int32, (tk, w), 1)
            s = jnp.where(qpos + diag >= kpos, s, -jnp.inf)
        m_prev = m_ref[hh, :, c0:c1]
        m_new = jnp.maximum(m_prev, jnp.max(s, axis=0, keepdims=True))
        alpha = jnp.exp2(m_prev - m_new)
        p = jnp.exp2(s - m_new).astype(BF16)
        acc_ref[hh, :, c0:c1] = alpha * acc_ref[hh, :, c0:c1] + _dot(v_ref[hh, j], p)
        m_ref[hh, :, c0:c1] = m_new

    def stage(j, slot):
        for hh in range(hg):
            for c0 in range(0, tq, cw):
                scores_head(hh, j + 1, 1 - slot, c0, c0 + cw)
                absorb_head(hh, j, slot, c0, c0 + cw)

    for hh in range(hg):
        scores_head(hh, 0, 0)

    def quad(jj, carry):
        for u in range(4):
            stage(4 * jj + u, u % 2)
        return carry

    lax.fori_loop(0, n_full // 4, quad, 0)

    @pl.when(n_full % 4 == 2)
    def _():
        stage(n_full - 2, 0)
        stage(n_full - 1, 1)
    for t in range(n_mask):
        lo = 0 if single_block else t * tk
        hi = min(lo + tk, tq)
        for hh in range(hg):
            if t + 1 < n_mask:
                scores_head(hh, n_full + t + 1, (t + 1) % 2, hi)
            absorb_head(hh, n_full + t, t % 2, lo, hi, diag=q_off if single_block else 0)
            if hi < tq:
                absorb_head(hh, n_full + t, t % 2, hi, tq)
    for g in range(hg // 2):
        halves = []
        for hh in (2 * g, 2 * g + 1):
            a = acc_ref[hh]
            halves.append(a[:HALF_LANES, :] / a[HALF_LANES:HALF_LANES + 1, :])
        o_ref[:, g * LANES:(g + 1) * LANES] = jnp.concatenate(halves, axis=0).T.astype(BF16)


def _fox_attn(qa, ka, vt, q_off):
    b, nh, _, seq_len = qa.shape
    n_keys = ka.shape[2]
    tk = vt.shape[-1]
    hg = ATTN_HEADS_PER_STEP
    tq = min(seq_len, ATTN_QUERY_TILE)
    assert seq_len % tq == 0 and tq % LANES == 0 and nh % hg == 0 and q_off + seq_len <= n_keys
    nq = seq_len // tq
    single_block = n_keys == tk
    assert (single_block and nq == 1) or (tq % (2 * tk) == 0 and q_off % (2 * tk) == 0)
    return pl.pallas_call(
        functools.partial(_fox_attn_kernel, hg=hg, tq=tq, tk=tk, q_off=q_off, single_block=single_block),
        out_shape=jax.ShapeDtypeStruct((b * seq_len, nh * HALF_LANES), BF16),
        grid=(b, nh // hg, nq),
        in_specs=[
            pl.BlockSpec((None, hg, LANES, tq), lambda bi, g, i: (bi, g, 0, i)),
            pl.BlockSpec((None, hg, n_keys, LANES), lambda bi, g, i: (bi, g, 0, 0)),
            pl.BlockSpec((None, hg, n_keys // tk, VT_ROWS, tk), lambda bi, g, i: (bi, g, 0, 0, 0)),
        ],
        out_specs=pl.BlockSpec((tq, hg * HALF_LANES), lambda bi, g, i: (bi * nq + i, g)),
        scratch_shapes=[pltpu.VMEM((hg, tk, tq), F32), pltpu.VMEM((hg, tk, tq), F32),
                        pltpu.VMEM((hg, VT_ROWS, tq), F32), pltpu.VMEM((hg, 1, tq), F32)],
        compiler_params=_cparams(("parallel", "parallel", "arbitrary")),
        name="fox_attn",
    )(qa, ka, vt)


def _trunk(x, pos0, chunk, s0, past, p):
    batch, seq_len, d = x.shape
    nh_ret, dk, dv = s0.shape[2:]
    rv = nh_ret * dv
    nh_fox = p['fox_b_f'].shape[0]
    width = p['fox_w_q'].shape[-1]
    pos = pos0 + jnp.arange(seq_len)
    h = x.reshape(batch * seq_len, d)

    h = _ffn(h, p['ffn1_g'][0], p['ffn1_w_in'][0], p['ffn1_w_out'][0])
    q, k, v, sg = _ret_proj(h, p['mix_g'][0], p['ret_w_in'][0], pos, seq_len, nh_ret, rv)
    o, s_fin = _retention(q, k, v, sg, s0[0], p['ret_gn_g'][0], batch, seq_len, chunk)
    h = _proj_res(o, p['ret_w_out'][0], h)
    h = _ffn(h, p['ffn2_g'][0], p['ffn2_w_in'][0], p['ffn2_w_out'][0])

    head_dim = width // nh_fox
    pad_q = -seq_len % LANES
    if past is None:
        assert pad_q == 0
        q_off = 0
        k_new, v_new, logf_pad, ka, vt, cq_t = _fox_kv(h, p['kv_g'], p['fox_w_kvf'], p['fox_b_f'], width, nh_fox,
                                                       batch, seq_len, tk=V7X_MXU_DIM)
    else:
        k_new, v_new, logf_pad = _fox_kv(h, p['kv_g'], p['fox_w_kvf'], p['fox_b_f'], width, nh_fox, batch, seq_len)
    k_new = k_new.reshape(batch, seq_len, nh_fox, head_dim)
    v_new = v_new.reshape(batch, seq_len, nh_fox, head_dim)
    logf_new = logf_pad.reshape(batch, seq_len, LANES)[:, :, :nh_fox]
    if past is not None:
        past_len = past[0].shape[1]
        k_all, v_all = (jnp.concatenate([old.reshape(batch, past_len, width), new.reshape(batch, seq_len, width)],
                                        axis=1) for old, new in ((past[0], k_new), (past[1], v_new)))
        logf_all = jnp.concatenate([past[2], logf_new], axis=1)
        n_keys = past_len + seq_len
        q_off = past_len
        pad_k = -n_keys % LANES
        assert pad_q <= pad_k
        if pad_k:
            k_all, v_all, logf_all = (jnp.pad(t, ((0, 0), (0, pad_k), (0, 0))) for t in (k_all, v_all, logf_all))
        cumf = _cumsum_rows(jnp.pad(logf_all, ((0, 0), (0, 0), (0, LANES - nh_fox))))
        cq_t = jnp.swapaxes(cumf[:, q_off:q_off + seq_len, :nh_fox], 1, 2)

    h = _ffn(h, p['ffn1_g'][1], p['ffn1_w_in'][1], p['ffn1_w_out'][1])
    qa = _fox_q(h, p['mix_g'][1], p['fox_w_q'][0], cq_t, batch, seq_len, nh_fox)
    if pad_q:
        qa = jnp.pad(qa, ((0, 0), (0, 0), (0, 0), (0, pad_q)))
    if past is None:
        o = _fox_attn(qa, ka, vt, q_off)
    else:
        o = _fox_attn_cached(qa, k_all, v_all, cumf, q_off, nh_fox)
    if pad_q:
        o = o.reshape(batch, seq_len + pad_q, width)[:, :seq_len].reshape(batch * seq_len, width)
    h = _proj_res(o, p['fox_w_out'][0], h)
    y = _ffn(h, p['ffn2_g'][1], p['ffn2_w_in'][1], p['ffn2_w_out'][1], final_g=p['final_g'])

    return (y.reshape(batch, seq_len, d), s_fin[None], k_new, v_new, logf_new)


def kernel(x_prompt, x_sample, state_ret, cache_k, cache_v, cache_logf, ffn1_g, ffn1_w_in, ffn1_w_out, mix_g,
           ffn2_g, ffn2_w_in, ffn2_w_out, ret_w_in, ret_gn_g, ret_w_out, kv_g, fox_w_kvf, fox_b_f, fox_w_q,
           fox_w_out, final_g):
    p = {'ffn1_g': ffn1_g, 'ffn1_w_in': ffn1_w_in, 'ffn1_w_out': ffn1_w_out, 'mix_g': mix_g,
         'ffn2_g': ffn2_g, 'ffn2_w_in': ffn2_w_in, 'ffn2_w_out': ffn2_w_out,
         'ret_w_in': ret_w_in, 'ret_gn_g': ret_gn_g, 'ret_w_out': ret_w_out,
         'kv_g': kv_g, 'fox_w_kvf': fox_w_kvf, 'fox_b_f': fox_b_f,
         'fox_w_q': fox_w_q, 'fox_w_out': fox_w_out, 'final_g': final_g}
    assert state_ret.shape[0] == 1 and fox_w_q.shape[0] == 1
    s0_prompt = jnp.zeros((1, x_prompt.shape[0]) + state_ret.shape[2:], F32)
    y_p, s_p, k_p, v_p, f_p = _trunk(x_prompt, 0, RET_CHUNK, s0_prompt, None, p)
    y_s, s_s, k_s, v_s, f_s = _trunk(x_sample, cache_k.shape[1], x_sample.shape[1], state_ret,
                                     (cache_k, cache_v, cache_logf), p)
    return (y_p, y_s, s_p, k_p, v_p, f_p, s_s, k_s, v_s, f_s)
```

```python
import functools

import jax
import jax.numpy as jnp
from jax import lax
from jax.experimental import pallas as pl
from jax.experimental.pallas import tpu as pltpu

F32 = jnp.float32
BF16 = jnp.bfloat16

EPS = 1e-6
ROPE_BASE = 10000.0
RET_CHUNK = 64
LANES = 128
HALF_LANES = LANES // 2
V7X_VMEM_LIMIT = 56 * 1024 * 1024
V7X_MXU_DIM = 256
ROW_TILE = 512
ROW_TILE_WIDE = 1024
ATTN_QUERY_TILE = 512
ATTN_HEADS_PER_STEP = 8

AUG_PIECE0 = 64
AUG_QPIECE0 = 112
N_PIECES = 3
VT_ROWS = 80
LOG2E = 1.4426950408889634


def _cparams(sem):
    return pltpu.CompilerParams(dimension_semantics=sem, vmem_limit_bytes=V7X_VMEM_LIMIT)


def _resident(shape):
    nd = len(shape)
    return pl.BlockSpec(shape, lambda *_: (0,) * nd, pipeline_mode=pl.Buffered(1))


def _rmsnorm(x, g):
    ms = jnp.mean(x * x, axis=-1, keepdims=True)
    return x * lax.rsqrt(ms + EPS) * g


def _silu(x):
    return x * jax.nn.sigmoid(x)


def _dot(a, b):
    return jnp.dot(a, b, preferred_element_type=F32)


def _dot_nt(a, b):
    return lax.dot_general(a, b, (((1,), (1,)), ((), ())), preferred_element_type=F32)


def _dot_tn(a, b):
    return lax.dot_general(a, b, (((0,), (0,)), ((), ())), preferred_element_type=F32)


def _split3(c):
    hi = c.astype(BF16).astype(F32)
    r = c - hi
    mid = r.astype(BF16).astype(F32)
    lo = (r - mid).astype(BF16).astype(F32)
    return hi, mid, lo


def _row_tile(n, want):
    t = min(n, want)
    assert n % t == 0 and t % 8 == 0, (n, t)
    return t


def _ffn_kernel(x_ref, g_ref, win_ref, wout_ref, fg_ref, *rest, d_ff, ck, final_norm, mixer_proj):
    if mixer_proj:
        a_ref, wp_ref, o_ref, act_ref = rest
        x = x_ref[...] + _dot(a_ref[...], wp_ref[...])
    else:
        o_ref, act_ref = rest
        x = x_ref[...]
    xn = _rmsnorm(x, g_ref[...]).astype(BF16)
    for c in range(d_ff // ck):
        gate = _dot(xn, win_ref[:, c * ck:(c + 1) * ck])
        up = _dot(xn, win_ref[:, d_ff + c * ck:d_ff + (c + 1) * ck])
        act_ref[:, c * ck:(c + 1) * ck] = (_silu(gate) * up).astype(BF16)
    y = x + 0.5 * _dot(act_ref[...], wout_ref[...])
    if final_norm:
        y = _rmsnorm(y, fg_ref[...])
    o_ref[...] = y


def _ffn(h, g, w_in, w_out, final_g=None, mixed=None, w_proj=None):
    n, d = h.shape
    d_ff = w_out.shape[0]
    tm = _row_tile(n, ROW_TILE_WIDE)
    ck = V7X_MXU_DIM
    assert d_ff % ck == 0
    fg = jnp.ones((d,), F32) if final_g is None else final_g
    args = [h, g.reshape(1, d), w_in.astype(BF16), w_out.astype(BF16), fg.reshape(1, d)]
    in_specs = [
        pl.BlockSpec((tm, d), lambda i: (i, 0)),
        _resident((1, d)),
        _resident((d, 2 * d_ff)),
        _resident((d_ff, d)),
        _resident((1, d)),
    ]
    if mixed is not None:
        k = mixed.shape[1]
        args += [mixed, w_proj.astype(BF16)]
        in_specs += [pl.BlockSpec((tm, k), lambda i: (i, 0)), _resident((k, d))]
    return pl.pallas_call(
        functools.partial(_ffn_kernel, d_ff=d_ff, ck=ck, final_norm=final_g is not None,
                          mixer_proj=mixed is not None),
        out_shape=jax.ShapeDtypeStruct((n, d), F32),
        grid=(n // tm,),
        in_specs=in_specs,
        out_specs=pl.BlockSpec((tm, d), lambda i: (i, 0)),
        scratch_shapes=[pltpu.VMEM((tm, d_ff), BF16)],
        compiler_params=_cparams(("parallel",)),
        name="ffn",
    )(*args)


def _proj_res_kernel(a_ref, w_ref, r_ref, o_ref):
    o_ref[...] = r_ref[...] + _dot(a_ref[...], w_ref[...])


def _proj_res(a, w, res):
    n, k = a.shape
    d = w.shape[1]
    tm = _row_tile(n, ROW_TILE_WIDE)
    return pl.pallas_call(
        _proj_res_kernel,
        out_shape=jax.ShapeDtypeStruct((n, d), F32),
        grid=(n // tm,),
        in_specs=[
            pl.BlockSpec((tm, k), lambda i: (i, 0)),
            _resident((k, d)),
            pl.BlockSpec((tm, d), lambda i: (i, 0)),
        ],
        out_specs=pl.BlockSpec((tm, d), lambda i: (i, 0)),
        compiler_params=_cparams(("parallel",)),
        name="proj_res",
    )(a, w.astype(BF16), res)


def _rope_tables(pos, half):
    inv = ROPE_BASE ** (-jnp.arange(half, dtype=F32) / half)
    ang = pos.astype(F32)[:, None] * inv[None, :]
    return jnp.cos(ang), jnp.sin(ang)


def _ret_proj_kernel(x_ref, g_ref, w_ref, cos_ref, sin_ref, q_ref, k_ref, v_ref, sg_ref, *, d, nh, rv):
    dk = d // nh
    half = dk // 2
    tm = x_ref.shape[0]
    ts = min(tm, V7X_MXU_DIM)
    norm = lambda r: _rmsnorm(x_ref[pl.ds(r * ts, ts), :], g_ref[...]).astype(BF16)
    xn = norm(0)
    for r in range(tm // ts):
        rows = pl.ds(r * ts, ts)
        cos = cos_ref[rows, :]
        sin = sin_ref[rows, :]
        sg_ref[rows, :] = _silu(_dot(xn, w_ref[:, 2 * d + rv:2 * d + 2 * rv])).astype(BF16)
        xn_next = norm(r + 1) if r + 1 < tm // ts else None
        for off, out_ref, scale in ((0, q_ref, dk ** -0.5), (d, k_ref, 1.0)):
            t = _dot(xn, w_ref[:, off:off + d])
            for hd in range(nh):
                x1 = t[:, hd * dk:hd * dk + half]
                x2 = t[:, hd * dk + half:(hd + 1) * dk]
                out_ref[rows, hd * dk:hd * dk + half] = ((x1 * cos - x2 * sin) * scale).astype(BF16)
                out_ref[rows, hd * dk + half:(hd + 1) * dk] = ((x1 * sin + x2 * cos) * scale).astype(BF16)
        v_ref[rows, :] = _dot(xn, w_ref[:, 2 * d:2 * d + rv]).astype(BF16)
        xn = xn_next


def _ret_proj(h, g, w_in, pos, seq_len, nh, rv):
    n, d = h.shape
    half = d // nh // 2
    tm = _row_tile(n, ROW_TILE)
    cos, sin = _rope_tables(pos, half)
    if seq_len % tm == 0:
        period = seq_len // tm
    else:
        assert tm % seq_len == 0
        cos, sin = (jnp.tile(t, (tm // seq_len, 1)) for t in (cos, sin))
        period = 1
    tab = pl.BlockSpec((tm, half), lambda i: (i % period, 0))
    row = lambda w: pl.BlockSpec((tm, w), lambda i: (i, 0))
    return pl.pallas_call(
        functools.partial(_ret_proj_kernel, d=d, nh=nh, rv=rv),
        out_shape=(jax.ShapeDtypeStruct((n, d), BF16), jax.ShapeDtypeStruct((n, d), BF16),
                   jax.ShapeDtypeStruct((n, rv), BF16), jax.ShapeDtypeStruct((n, rv), BF16)),
        grid=(n // tm,),
        in_specs=[row(d), _resident((1, d)), _resident((d, 2 * d + 2 * rv)), tab, tab],
        out_specs=(row(d), row(d), row(rv), row(rv)),
        compiler_params=_cparams(("parallel",)),
        name="ret_proj",
    )(h, g.reshape(1, d), w_in.astype(BF16), cos, sin)


def _ret_decay_tables(nh, t, chunk):
    log_gamma = jnp.log1p(-jnp.exp2(-5.0 - jnp.arange(nh, dtype=F32)))
    pos = jnp.arange(t, dtype=F32)
    dist = jnp.abs(pos[:, None] - pos[None, :])
    cid = jnp.arange(t) // chunk
    visible = (cid[None, :] <= cid[:, None]).astype(F32)
    dmask = jnp.exp(log_gamma[:, None, None] * dist) * visible[None]
    qdec = jnp.exp(log_gamma[:, None] * (pos[None, :] + 1.0))[:, :, None]
    kdec = jnp.exp(log_gamma[:, None] * (t - 1.0 - pos[None, :]))[:, :, None]
    sdec = jnp.exp(log_gamma * t)
    return dmask, qdec, kdec, sdec


def _retention_kernel(sdec_ref, q_ref, k_ref, v_ref, sg_ref, dmask_ref, qdec_ref, kdec_ref, gn_ref, s0_ref,
                      o_ref, sout_ref, s_ref, *, nh, dk, dv):
    j = pl.program_id(1)

    @pl.when(j == 0)
    def _():
        s_ref[...] = s0_ref[...]

    for hd in range(nh):
        qh = q_ref[:, hd * dk:(hd + 1) * dk]
        kh = k_ref[:, hd * dk:(hd + 1) * dk]
        vh = v_ref[:, hd * dv:(hd + 1) * dv]
        state = s_ref[hd]
        scores = _dot_nt(qh, kh) * dmask_ref[hd]
        o = _dot(scores.astype(BF16), vh) + _dot(qh, state.astype(BF16)) * qdec_ref[hd]
        kd = (kh.astype(F32) * kdec_ref[hd]).astype(BF16)
        s_ref[hd] = sdec_ref[hd] * state + _dot_tn(kd, vh)
        mu = jnp.mean(o, axis=-1, keepdims=True)
        oc = o - mu
        var = jnp.mean(oc * oc, axis=-1, keepdims=True)
        on = oc * lax.rsqrt(var + EPS) * gn_ref[:, hd * dv:(hd + 1) * dv]
        o_ref[:, hd * dv:(hd + 1) * dv] = (sg_ref[:, hd * dv:(hd + 1) * dv].astype(F32) * on).astype(BF16)

    @pl.when(j == pl.num_programs(1) - 1)
    def _():
        sout_ref[...] = s_ref[...]


def _retention(q, k, v, sg, s0, gn_g, batch, seq_len, chunk):
    nh, dk, dv = s0.shape[1:]
    rv = nh * dv
    t = _row_tile(seq_len, max(chunk, V7X_MXU_DIM))
    assert t % chunk == 0
    nblk = seq_len // t
    dmask, qdec, kdec, sdec = _ret_decay_tables(nh, t, chunk)
    row = lambda w: pl.BlockSpec((t, w), lambda b, j: (b * nblk + j, 0))
    st = pl.BlockSpec((None, nh, dk, dv), lambda b, j: (b, 0, 0, 0))
    return pl.pallas_call(
        functools.partial(_retention_kernel, nh=nh, dk=dk, dv=dv),
        out_shape=(jax.ShapeDtypeStruct((batch * seq_len, rv), BF16),
                   jax.ShapeDtypeStruct((batch, nh, dk, dv), F32)),
        grid=(batch, nblk),
        in_specs=[
            pl.BlockSpec(memory_space=pltpu.SMEM),
            row(nh * dk), row(nh * dk), row(rv), row(rv),
            _resident((nh, t, t)), _resident((nh, t, 1)), _resident((nh, t, 1)), _resident((1, rv)),
            st,
        ],
        out_specs=(row(rv), st),
        scratch_shapes=[pltpu.VMEM((nh, dk, dv), F32)],
        compiler_params=_cparams(("parallel", "arbitrary")),
        name="retention",
    )(sdec, q, k, v, sg, dmask, qdec, kdec, gn_g.reshape(1, rv), s0)


def _lane_iota(rows):
    return lax.broadcasted_iota(jnp.int32, (rows, LANES), 1)


def _head_lanes(x, hd):
    t = x[:, (hd // 2) * LANES:(hd // 2 + 1) * LANES]
    return pltpu.roll(t, HALF_LANES, axis=1) if hd % 2 else t


def _tri_cumsum(tri, x):
    hi, mid, lo = _split3(x)
    return _dot(tri, hi.astype(BF16)) + _dot(tri, mid.astype(BF16)) + _dot(tri, lo.astype(BF16))


def _pack_keys(k, c, ka_ref, nh):
    rows = k.shape[0]
    lane = _lane_iota(rows)
    hi, mid, lo = _split3(c * LOG2E)
    extra = jnp.where(lane < AUG_QPIECE0 + N_PIECES, 1.0, 0.0)
    for p, piece in reversed(list(enumerate((hi, mid, lo)))):
        lo_lane = AUG_PIECE0 + p * nh
        extra = jnp.where(lane < lo_lane + nh, -pltpu.roll(piece, lo_lane, axis=1), extra)
    for hd in range(nh):
        ka_ref[hd] = jnp.where(lane < HALF_LANES, _head_lanes(k, hd), extra).astype(BF16)


def _pack_values(v, vt_ref, nh, tk):
    rows = v.shape[0]
    vt = v.T
    sub = lax.broadcasted_iota(jnp.int32, (VT_ROWS - HALF_LANES, rows), 0)
    ones_row = jnp.where(sub == 0, 1.0, 0.0)
    for hd in range(nh):
        t = jnp.concatenate([vt[hd * HALF_LANES:(hd + 1) * HALF_LANES, :], ones_row], axis=0).astype(BF16)
        for s in range(rows // tk):
            vt_ref[hd, s] = t[:, s * tk:(s + 1) * tk]


def _fox_kv_kernel(x_ref, g_ref, wk_ref, wv_ref, wf_ref, bf_ref, tri_ref, k_ref, v_ref, logf_ref, *rest,
                   nh, tk, packed):
    tm = x_ref.shape[0]
    ts = tri_ref.shape[0]
    if packed:
        ka_ref, vt_ref, ct_ref, carry_ref = rest

        @pl.when(pl.program_id(1) == 0)
        def _():
            carry_ref[...] = jnp.zeros_like(carry_ref)

    def project(r):
        rows = pl.ds(r * ts, ts)
        xn = _rmsnorm(x_ref[rows, :], g_ref[...]).astype(BF16)
        k = _dot(xn, wk_ref[...])
        v = _dot(xn, wv_ref[...])
        z = _dot(xn, wf_ref[...]) + bf_ref[...]
        return k, v, jnp.minimum(z, 0.0) - jnp.log1p(jnp.exp(-jnp.abs(z)))

    def emit(r, k, v, logf):
        logf_ref[pl.ds(r * ts, ts), :] = logf
        for src, dst in ((k, k_ref), (v, v_ref)):
            for hd in range(nh):
                dst[pl.ds(r * ts * nh + hd, ts, stride=nh), :] = _head_lanes(src, hd)[:, :HALF_LANES]
        if packed:
            c = _tri_cumsum(tri_ref[...], jnp.where(_lane_iota(ts) < nh, logf, 0.0)) + carry_ref[...]
            carry_ref[...] = c[ts - 1:ts, :]
            ct_ref[:, pl.ds(r * ts, ts)] = c.T[:nh, :]
            _pack_keys(k, c, ka_ref.at[:, pl.ds(r * ts, ts), :], nh)
            _pack_values(v, vt_ref.at[:, pl.ds(r * (ts // tk), ts // tk)], nh, tk)

    pending = project(0)
    for r in range(tm // ts):
        done = pending
        if r + 1 < tm // ts:
            pending = project(r + 1)
        emit(r, *done)


def _fox_kv(h, g, w_kvf, b_f, width, nh, batch, seq_len, tk=None):
    n, d = h.shape
    packed = tk is not None
    tm = _row_tile(seq_len if packed else n, ROW_TILE)
    grid = (batch, seq_len // tm) if packed else (n // tm, 1)
    nblk = grid[1]
    assert width == nh * HALF_LANES and N_PIECES * nh <= AUG_QPIECE0 - AUG_PIECE0
    assert nh & (nh - 1) == 0 and AUG_PIECE0 % nh == 0
    wf = jnp.zeros((d, LANES), BF16).at[:, :nh].set(w_kvf[:, 2 * width:].astype(BF16))
    bf = jnp.zeros((1, LANES), F32).at[0, :nh].set(b_f)
    ts = min(tm, V7X_MXU_DIM)
    assert tm % ts == 0
    tri = (jnp.arange(ts)[:, None] >= jnp.arange(ts)[None, :]).astype(BF16)
    row = lambda r, w: pl.BlockSpec((r, w), lambda i, j: (i * nblk + j, 0))
    out_shape = [jax.ShapeDtypeStruct((n * nh, HALF_LANES), F32), jax.ShapeDtypeStruct((n * nh, HALF_LANES), F32),
                 jax.ShapeDtypeStruct((n, LANES), F32)]
    out_specs = [row(tm * nh, HALF_LANES), row(tm * nh, HALF_LANES), row(tm, LANES)]
    scratch = []
    if packed:
        assert ts % tk == 0
        out_shape += [jax.ShapeDtypeStruct((batch, nh, seq_len, LANES), BF16),
                      jax.ShapeDtypeStruct((batch, nh, seq_len // tk, VT_ROWS, tk), BF16),
                      jax.ShapeDtypeStruct((batch, nh, seq_len), F32)]
        out_specs += [pl.BlockSpec((None, nh, tm, LANES), lambda i, j: (i, 0, j, 0)),
                      pl.BlockSpec((None, nh, tm // tk, VT_ROWS, tk), lambda i, j: (i, 0, j, 0, 0)),
                      pl.BlockSpec((None, nh, tm), lambda i, j: (i, 0, j))]
        scratch = [pltpu.VMEM((1, LANES), F32)]
    return pl.pallas_call(
        functools.partial(_fox_kv_kernel, nh=nh, tk=tk, packed=packed),
        out_shape=tuple(out_shape),
        grid=grid,
        in_specs=[row(tm, d), _resident((1, d)), _resident((d, width)), _resident((d, width)),
                  _resident((d, LANES)), _resident((1, LANES)), _resident((ts, ts))],
        out_specs=tuple(out_specs),
        scratch_shapes=scratch,
        compiler_params=_cparams(("parallel", "arbitrary")),
        name="fox_kv",
    )(h, g.reshape(1, d), w_kvf[:, :width].astype(BF16), w_kvf[:, width:2 * width].astype(BF16), wf, bf, tri)


def _cumsum_kernel(x_ref, tri_ref, o_ref, *, tc, nblk):
    tri = tri_ref[...]

    def body(i, carry):
        r0 = pl.multiple_of(i * tc, tc)
        c = _tri_cumsum(tri, x_ref[pl.ds(r0, tc), :]) + carry
        o_ref[pl.ds(r0, tc), :] = c
        return c[tc - 1:tc, :]

    lax.fori_loop(0, nblk, body, jnp.zeros((1, x_ref.shape[-1]), F32))


def _cumsum_rows(x):
    b, n, w = x.shape
    tc = LANES if n % LANES == 0 else HALF_LANES
    assert n % tc == 0
    tri = (jnp.arange(tc)[:, None] >= jnp.arange(tc)[None, :]).astype(BF16)
    blk = pl.BlockSpec((None, n, w), lambda i: (i, 0, 0))
    return pl.pallas_call(
        functools.partial(_cumsum_kernel, tc=tc, nblk=n // tc),
        out_shape=jax.ShapeDtypeStruct((b, n, w), F32),
        grid=(b,),
        in_specs=[blk, _resident((tc, tc))],
        out_specs=blk,
        compiler_params=_cparams(("parallel",)),
        name="cumsum",
    )(x, tri)


def _fox_attn_cached_kernel(q_ref, k_ref, v_ref, c_ref, o_ref, ka_ref, vt_ref, *, nh, q_off):
    n = k_ref.shape[0]
    tq = q_ref.shape[-1]
    _pack_keys(k_ref[...], c_ref[...], ka_ref, nh)
    _pack_values(v_ref[...], vt_ref, nh, n)
    kpos = lax.broadcasted_iota(jnp.int32, (n, tq), 0)
    qpos = lax.broadcasted_iota(jnp.int32, (n, tq), 1)
    visible = qpos + q_off >= kpos
    scores = lambda hh: _dot(ka_ref[hh], q_ref[hh])
    halves = []
    pending = scores(0)
    for hh in range(nh):
        s = jnp.where(visible, pending, -jnp.inf)
        if hh + 1 < nh:
            pending = scores(hh + 1)
        p = jnp.exp2(s - jnp.max(s, axis=0, keepdims=True)).astype(BF16)
        a = _dot(vt_ref[hh, 0], p)
        halves.append(a[:HALF_LANES, :] / a[HALF_LANES:HALF_LANES + 1, :])
        if hh % 2:
            o_ref[:, (hh // 2) * LANES:(hh // 2 + 1) * LANES] = jnp.concatenate(halves, axis=0).T.astype(BF16)
            halves = []


def _fox_attn_cached(qa, k, v, c, q_off, nh):
    b, n, width = k.shape
    tq = qa.shape[-1]
    assert n % LANES == 0 and tq % LANES == 0 and q_off + tq <= n
    blk = lambda w: pl.BlockSpec((None, n, w), lambda i: (i, 0, 0))
    return pl.pallas_call(
        functools.partial(_fox_attn_cached_kernel, nh=nh, q_off=q_off),
        out_shape=jax.ShapeDtypeStruct((b * tq, width), BF16),
        grid=(b,),
        in_specs=[pl.BlockSpec((None, nh, LANES, tq), lambda i: (i, 0, 0, 0)), blk(width), blk(width), blk(LANES)],
        out_specs=pl.BlockSpec((tq, width), lambda i: (i, 0)),
        scratch_shapes=[pltpu.VMEM((nh, n, LANES), BF16), pltpu.VMEM((nh, 1, VT_ROWS, n), BF16)],
        compiler_params=_cparams(("parallel",)),
        name="fox_attn_cached",
    )(qa, k, v, c)


def _fox_q_kernel(x_ref, g_ref, wt_ref, ct_ref, qa_ref, *, nh):
    tm = x_ref.shape[0]
    xn = _rmsnorm(x_ref[...], g_ref[...]).astype(BF16)
    qt = _dot_nt(wt_ref[...], xn) * (HALF_LANES ** -0.5 * LOG2E)
    hi, mid, lo = _split3(ct_ref[...] * LOG2E)
    sub = lax.broadcasted_iota(jnp.int32, (8, tm), 0)
    rowp = lax.broadcasted_iota(jnp.int32, (N_PIECES * nh, tm), 0)
    tail = jnp.zeros((LANES - AUG_QPIECE0 - 8, tm), F32)
    for hd in range(nh):
        onehot = jnp.where((rowp & (nh - 1)) == hd, 1.0, 0.0)
        own = [jnp.broadcast_to(a[hd:hd + 1, :], (8, tm)) for a in (hi, mid, lo)]
        pieces = jnp.where(sub == 0, own[0], jnp.where(sub == 1, own[1], jnp.where(sub == 2, own[2], 0.0)))
        tile = jnp.concatenate([qt[hd * HALF_LANES:(hd + 1) * HALF_LANES, :], onehot, pieces, tail], axis=0)
        qa_ref[hd] = tile.astype(BF16)


def _fox_q(h, g, w_q, cq_t, batch, seq_len, nh):
    n, d = h.shape
    tm = _row_tile(seq_len, ROW_TILE_WIDE)
    nblk = seq_len // tm
    return pl.pallas_call(
        functools.partial(_fox_q_kernel, nh=nh),
        out_shape=jax.ShapeDtypeStruct((batch, nh, LANES, seq_len), BF16),
        grid=(batch, nblk),
        in_specs=[
            pl.BlockSpec((tm, d), lambda b, j: (b * nblk + j, 0)),
            _resident((1, d)),
            _resident((nh * HALF_LANES, d)),
            pl.BlockSpec((None, nh, tm), lambda b, j: (b, 0, j)),
        ],
        out_specs=pl.BlockSpec((None, nh, LANES, tm), lambda b, j: (b, 0, 0, j)),
        compiler_params=_cparams(("parallel", "parallel")),
        name="fox_q",
    )(h, g.reshape(1, d), w_q.T.astype(BF16), cq_t)


def _fox_attn_kernel(q_ref, k_ref, v_ref, o_ref, s0_ref, s1_ref, acc_ref, m_ref, *, hg, tq, tk, q_off,
                     single_block):
    i = pl.program_id(2)
    q0 = q_off + i * tq
    n_full, n_mask = (0, 1) if single_block else (q0 // tk, tq // tk)
    acc_ref[...] = jnp.zeros_like(acc_ref)
    m_ref[...] = jnp.full_like(m_ref, -jnp.inf)
    slots = (s0_ref, s1_ref)
    cw = min(tq, V7X_MXU_DIM)

    def scores_head(hh, j, slot, c0=0, c1=tq):
        r0 = pl.multiple_of(j * tk, tk)
        slots[slot][hh, :, c0:c1] = _dot(k_ref[hh, pl.ds(r0, tk), :], q_ref[hh, :, c0:c1])

    def absorb_head(hh, j, slot, c0=0, c1=tq, diag=None):
        w = c1 - c0
        s = slots[slot][hh, :, c0:c1]
        if diag is not None:
            kpos = lax.broadcasted_iota(jnp.int32, (tk, w), 0)
            qpos = lax.broadcasted_iota(jnp.int32, (tk, w), 1)
            s = jnp.where(qpos + diag >= kpos, s, -jnp.inf)
        m_prev = m_ref[hh, :, c0:c1]
        m_new = jnp.maximum(m_prev, jnp.max(s, axis=0, keepdims=True))
        alpha = jnp.exp2(m_prev - m_new)
        p = jnp.exp2(s - m_new).astype(BF16)
        acc_ref[hh, :, c0:c1] = alpha * acc_ref[hh, :, c0:c1] + _dot(v_ref[hh, j], p)
        m_ref[hh, :, c0:c1] = m_new

    def stage(j, slot):
        for hh in range(hg):
            for c0 in range(0, tq, cw):
                scores_head(hh, j + 1, 1 - slot, c0, c0 + cw)
                absorb_head(hh, j, slot, c0, c0 + cw)

    for hh in range(hg):
        scores_head(hh, 0, 0)

    def quad(jj, carry):
        for u in range(4):
            stage(4 * jj + u, u % 2)
        return carry

    lax.fori_loop(0, n_full // 4, quad, 0)

    @pl.when(n_full % 4 == 2)
    def _():
        stage(n_full - 2, 0)
        stage(n_full - 1, 1)
    for t in range(n_mask):
        lo = 0 if single_block else t * tk
        hi = min(lo + tk, tq)
        for hh in range(hg):
            if t + 1 < n_mask:
                scores_head(hh, n_full + t + 1, (t + 1) % 2, hi)
            absorb_head(hh, n_full + t, t % 2, lo, hi, diag=q_off if single_block else 0)
            if hi < tq:
                absorb_head(hh, n_full + t, t % 2, hi, tq)
    for g in range(hg // 2):
        halves = []
        for hh in (2 * g, 2 * g + 1):
            a = acc_ref[hh]
            halves.append(a[:HALF_LANES, :] / a[HALF_LANES:HALF_LANES + 1, :])
        o_ref[:, g * LANES:(g + 1) * LANES] = jnp.concatenate(halves, axis=0).T.astype(BF16)


def _fox_attn(qa, ka, vt, q_off):
    b, nh, _, seq_len = qa.shape
    n_keys = ka.shape[2]
    tk = vt.shape[-1]
    hg = ATTN_HEADS_PER_STEP
    tq = min(seq_len, ATTN_QUERY_TILE)
    assert seq_len % tq == 0 and tq % LANES == 0 and nh % hg == 0 and q_off + seq_len <= n_keys
    nq = seq_len // tq
    single_block = n_keys == tk
    assert (single_block and nq == 1) or (tq % (2 * tk) == 0 and q_off % (2 * tk) == 0)
    return pl.pallas_call(
        functools.partial(_fox_attn_kernel, hg=hg, tq=tq, tk=tk, q_off=q_off, single_block=single_block),
        out_shape=jax.ShapeDtypeStruct((b * seq_len, nh * HALF_LANES), BF16),
        grid=(b, nh // hg, nq),
        in_specs=[
            pl.BlockSpec((None, hg, LANES, tq), lambda bi, g, i: (bi, g, 0, i)),
            pl.BlockSpec((None, hg, n_keys, LANES), lambda bi, g, i: (bi, g, 0, 0)),
            pl.BlockSpec((None, hg, n_keys // tk, VT_ROWS, tk), lambda bi, g, i: (bi, g, 0, 0, 0)),
        ],
        out_specs=pl.BlockSpec((tq, hg * HALF_LANES), lambda bi, g, i: (bi * nq + i, g)),
        scratch_shapes=[pltpu.VMEM((hg, tk, tq), F32), pltpu.VMEM((hg, tk, tq), F32),
                        pltpu.VMEM((hg, VT_ROWS, tq), F32), pltpu.VMEM((hg, 1, tq), F32)],
        compiler_params=_cparams(("parallel", "parallel", "arbitrary")),
        name="fox_attn",
    )(qa, ka, vt)


def _trunk(x, pos0, chunk, s0, past, p):
    batch, seq_len, d = x.shape
    nh_ret, dk, dv = s0.shape[2:]
    rv = nh_ret * dv
    nh_fox = p['fox_b_f'].shape[0]
    width = p['fox_w_q'].shape[-1]
    pos = pos0 + jnp.arange(seq_len)
    h = x.reshape(batch * seq_len, d)

    h = _ffn(h, p['ffn1_g'][0], p['ffn1_w_in'][0], p['ffn1_w_out'][0])
    q, k, v, sg = _ret_proj(h, p['mix_g'][0], p['ret_w_in'][0], pos, seq_len, nh_ret, rv)
    o, s_fin = _retention(q, k, v, sg, s0[0], p['ret_gn_g'][0], batch, seq_len, chunk)
    h = _ffn(h, p['ffn2_g'][0], p['ffn2_w_in'][0], p['ffn2_w_out'][0], mixed=o, w_proj=p['ret_w_out'][0])

    head_dim = width // nh_fox
    pad_q = -seq_len % LANES
    if past is None:
        assert pad_q == 0
        q_off = 0
        k_new, v_new, logf_pad, ka, vt, cq_t = _fox_kv(h, p['kv_g'], p['fox_w_kvf'], p['fox_b_f'], width, nh_fox,
                                                       batch, seq_len, tk=V7X_MXU_DIM)
    else:
        k_new, v_new, logf_pad = _fox_kv(h, p['kv_g'], p['fox_w_kvf'], p['fox_b_f'], width, nh_fox, batch, seq_len)
    k_new = k_new.reshape(batch, seq_len, nh_fox, head_dim)
    v_new = v_new.reshape(batch, seq_len, nh_fox, head_dim)
    logf_new = logf_pad.reshape(batch, seq_len, LANES)[:, :, :nh_fox]
    if past is not None:
        past_len = past[0].shape[1]
        k_all, v_all = (jnp.concatenate([old.reshape(batch, past_len, width), new.reshape(batch, seq_len, width)],
                                        axis=1) for old, new in ((past[0], k_new), (past[1], v_new)))
        logf_all = jnp.concatenate([past[2], logf_new], axis=1)
        n_keys = past_len + seq_len
        q_off = past_len
        pad_k = -n_keys % LANES
        assert pad_q <= pad_k
        if pad_k:
            k_all, v_all, logf_all = (jnp.pad(t, ((0, 0), (0, pad_k), (0, 0))) for t in (k_all, v_all, logf_all))
        cumf = _cumsum_rows(jnp.pad(logf_all, ((0, 0), (0, 0), (0, LANES - nh_fox))))
        cq_t = jnp.swapaxes(cumf[:, q_off:q_off + seq_len, :nh_fox], 1, 2)

    h = _ffn(h, p['ffn1_g'][1], p['ffn1_w_in'][1], p['ffn1_w_out'][1])
    qa = _fox_q(h, p['mix_g'][1], p['fox_w_q'][0], cq_t, batch, seq_len, nh_fox)
    if pad_q:
        qa = jnp.pad(qa, ((0, 0), (0, 0), (0, 0), (0, pad_q)))
    if past is None:
        o = _fox_attn(qa, ka, vt, q_off)
    else:
        o = _fox_attn_cached(qa, k_all, v_all, cumf, q_off, nh_fox)
    if pad_q:
        o = o.reshape(batch, seq_len + pad_q, width)[:, :seq_len].reshape(batch * seq_len, width)
    y = _ffn(h, p['ffn2_g'][1], p['ffn2_w_in'][1], p['ffn2_w_out'][1], final_g=p['final_g'],
             mixed=o, w_proj=p['fox_w_out'][0])

    return (y.reshape(batch, seq_len, d), s_fin[None], k_new, v_new, logf_new)


def kernel(x_prompt, x_sample, state_ret, cache_k, cache_v, cache_logf, ffn1_g, ffn1_w_in, ffn1_w_out, mix_g,
           ffn2_g, ffn2_w_in, ffn2_w_out, ret_w_in, ret_gn_g, ret_w_out, kv_g, fox_w_kvf, fox_b_f, fox_w_q,
           fox_w_out, final_g):
    p = {'ffn1_g': ffn1_g, 'ffn1_w_in': ffn1_w_in, 'ffn1_w_out': ffn1_w_out, 'mix_g': mix_g,
         'ffn2_g': ffn2_g, 'ffn2_w_in': ffn2_w_in, 'ffn2_w_out': ffn2_w_out,
         'ret_w_in': ret_w_in, 'ret_gn_g': ret_gn_g, 'ret_w_out': ret_w_out,
         'kv_g': kv_g, 'fox_w_kvf': fox_w_kvf, 'fox_b_f': fox_b_f,
         'fox_w_q': fox_w_q, 'fox_w_out': fox_w_out, 'final_g': final_g}
    assert state_ret.shape[0] == 1 and fox_w_q.shape[0] == 1
    s0_prompt = jnp.zeros((1, x_prompt.shape[0]) + state_ret.shape[2:], F32)
    y_p, s_p, k_p, v_p, f_p = _trunk(x_prompt, 0, RET_CHUNK, s0_prompt, None, p)
    y_s, s_s, k_s, v_s, f_s = _trunk(x_sample, cache_k.shape[1], x_sample.shape[1], state_ret,
                                     (cache_k, cache_v, cache_logf), p)
    return (y_p, y_s, s_p, k_p, v_p, f_p, s_s, k_s, v_s, f_s)
```
